```python
import math
import jax, jax.numpy as jnp
from jax import lax
import numpy as np

D_MODEL = 2048
BATCH = 1
SEQ = 16384
DEPTH = 1

D_MIX = D_MODEL
ATTN_WIDTH = D_MIX // 2
REC_WIDTH = D_MIX - ATTN_WIDTH
DA_HEAD_DIM = 128
DA_HEADS = ATTN_WIDTH // (2 * DA_HEAD_DIM)
REC_BLOCKS = 8
REC_BLOCK_DIM = REC_WIDTH // REC_BLOCKS
CONV_W = 4
RG_C = 8.0
IN_PROJ_DIM = 3 * ATTN_WIDTH + 2 * REC_WIDTH
Q_BLOCK = 128
N_MEM = 256
CROSS_HEADS = 4
CROSS_HEAD_DIM = D_MODEL // CROSS_HEADS
N_EXPERTS = 32
TOP_K = 4
D_FF = D_MODEL
SWIGLU_LIMIT = 7.0
SWIGLU_ALPHA = 1.702
MOE_BLOCK = 128
EPS = 1e-6
DA_EPS = 1e-5

kernel_name = "hymba_diffattn_rglru_moe_layer"


def rmsnorm(x, g, eps=EPS):
    x32 = x.astype(jnp.float32)
    y = x32 * lax.rsqrt(jnp.mean(x32 * x32, axis=-1, keepdims=True) + eps)
    return (y * g.astype(jnp.float32)).astype(x.dtype)


def diff_attention(q, k, v, lq1, lk1, lq2, lk2, subln_g, lambda_init):
    B, S, _ = q.shape
    nb = S // Q_BLOCK
    out_dtype = q.dtype
    qb = q.reshape(B, nb, Q_BLOCK, DA_HEADS, 2, DA_HEAD_DIM).transpose(1, 0, 3, 4, 2, 5).astype(jnp.float32)
    kh = k.reshape(B, S, DA_HEADS, 2, DA_HEAD_DIM).transpose(0, 2, 3, 1, 4).astype(jnp.float32)
    vh = v.reshape(B, S, DA_HEADS, 2 * DA_HEAD_DIM).transpose(0, 2, 1, 3).astype(jnp.float32)
    f32 = jnp.float32
    lam = (jnp.exp(jnp.sum(lq1.astype(f32) * lk1.astype(f32)))
           - jnp.exp(jnp.sum(lq2.astype(f32) * lk2.astype(f32))) + lambda_init)
    scale = DA_HEAD_DIM ** -0.5
    k_pos = jnp.arange(S)

    def one_block(args):
        q_blk, blk = args
        q_pos = blk * Q_BLOCK + jnp.arange(Q_BLOCK)
        s = jnp.einsum('bhiqd,bhikd->bhiqk', q_blk, kh) * scale
        s = jnp.where(k_pos[None, :] <= q_pos[:, None], s, -jnp.inf)
        p = jax.nn.softmax(s, axis=-1)
        a = p[:, :, 0] - lam * p[:, :, 1]
        return jnp.einsum('bhqk,bhkd->bhqd', a, vh)

    o = lax.map(one_block, (qb, jnp.arange(nb)))
    o = o.transpose(1, 0, 3, 2, 4).reshape(B, S, DA_HEADS, 2 * DA_HEAD_DIM)
    o = rmsnorm(o, subln_g, DA_EPS) * (1.0 - lambda_init)
    return o.reshape(B, S, ATTN_WIDTH).astype(out_dtype)


def rg_lru_group(xr, xg, conv_w, conv_b, w_a, b_a, w_x, b_x, rg_lambda, rec_norm_g):
    B, S, C = xr.shape
    xc = lax.conv_general_dilated(
        xr, conv_w[:, None, :].astype(xr.dtype), window_strides=(1,),
        padding=[(CONV_W - 1, 0)], dimension_numbers=('NWC', 'WIO', 'NWC'),
        feature_group_count=C) + conv_b
    xb = xc.reshape(B, S, REC_BLOCKS, REC_BLOCK_DIM)
    r = jax.nn.sigmoid(jnp.einsum('bsnc,ncd->bsnd', xb, w_a).reshape(B, S, C) + b_a)
    i = jax.nn.sigmoid(jnp.einsum('bsnc,ncd->bsnd', xb, w_x).reshape(B, S, C) + b_x)
    log_a = -RG_C * jax.nn.softplus(-rg_lambda.astype(jnp.float32)) * r.astype(jnp.float32)
    a = jnp.exp(log_a)
    b = jnp.sqrt(-jnp.expm1(2.0 * log_a)) * (i * xc).astype(jnp.float32)

    def combine(left, right):
        a1, b1 = left
        a2, b2 = right
        return a1 * a2, a2 * b1 + b2

    _, h = lax.associative_scan(combine, (a, b), axis=1)
    y = h.astype(xr.dtype) * jax.nn.gelu(xg)
    return rmsnorm(y, rec_norm_g)


def cross_attention(h, mem_n, w_cq, w_ckv, w_co):
    B, S, D = h.shape
    M = mem_n.shape[1]
    q = (h @ w_cq).reshape(B, S, CROSS_HEADS, CROSS_HEAD_DIM)
    kv = mem_n @ w_ckv
    k = kv[..., :D].reshape(B, M, CROSS_HEADS, CROSS_HEAD_DIM)
    v = kv[..., D:].reshape(B, M, CROSS_HEADS, CROSS_HEAD_DIM)
    s = jnp.einsum('bshd,bmhd->bhsm', q, k).astype(jnp.float32) * (CROSS_HEAD_DIM ** -0.5)
    p = jax.nn.softmax(s, axis=-1).astype(v.dtype)
    o = jnp.einsum('bhsm,bmhd->bshd', p, v).reshape(B, S, D)
    return o @ w_co


def moe(h, w_router, b_router, w_gu, b_gu, w_down, b_down):
    B, S, D = h.shape
    T = B * S
    M = T * TOP_K
    xf = h.reshape(T, D)
    logits = jnp.dot(xf, w_router).astype(jnp.float32) + b_router.astype(jnp.float32)
    top_vals, top_idx = lax.top_k(logits, TOP_K)
    gates = jax.nn.softmax(top_vals, axis=-1)
    flat_e = top_idx.reshape(M)
    order = jnp.argsort(flat_e)
    e_sorted = flat_e[order]
    tok_sorted = order // TOP_K
    g_sorted = gates.reshape(M)[order]
    counts = jnp.bincount(flat_e, length=N_EXPERTS)
    padded = ((counts + MOE_BLOCK - 1) // MOE_BLOCK) * MOE_BLOCK
    starts = jnp.cumsum(counts) - counts
    pad_ends = jnp.cumsum(padded)
    pad_starts = pad_ends - padded
    dest = pad_starts[e_sorted] + jnp.arange(M) - starts[e_sorted]
    m_pad = M + N_EXPERTS * MOE_BLOCK
    n_blk = m_pad // MOE_BLOCK
    row_tok = jnp.zeros((m_pad,), jnp.int32).at[dest].set(tok_sorted.astype(jnp.int32))
    row_gate = jnp.zeros((m_pad,), jnp.float32).at[dest].set(g_sorted)
    blk_e = jnp.minimum(jnp.searchsorted(pad_ends, jnp.arange(n_blk) * MOE_BLOCK, side='right'),
                        N_EXPERTS - 1)

    def step(acc, args):
        tok, g, e = args
        xb = xf[tok]
        gu = xb @ w_gu[e] + b_gu[e]
        gate = jnp.minimum(gu[:, :D_FF], SWIGLU_LIMIT)
        up = jnp.clip(gu[:, D_FF:], -SWIGLU_LIMIT, SWIGLU_LIMIT)
        act = (up + 1.0) * gate * jax.nn.sigmoid(SWIGLU_ALPHA * gate)
        y = act @ w_down[e] + b_down[e]
        return acc.at[tok].add(y * g[:, None].astype(y.dtype)), None

    acc, _ = lax.scan(step, jnp.zeros_like(xf),
                      (row_tok.reshape(n_blk, MOE_BLOCK), row_gate.reshape(n_blk, MOE_BLOCK), blk_e))
    return acc.reshape(B, S, D)


def setup_inputs(seed: int = 0) -> dict:
    key = jax.random.key(seed)
    ks = iter(jax.random.split(key, 40))
    f32 = jnp.float32

    def nrm(shape, scale):
        return jax.random.normal(next(ks), shape, f32) * scale

    def gain(shape):
        return 1.0 + 0.02 * jax.random.normal(next(ks), shape, f32)

    L = DEPTH
    x = nrm((BATCH, SEQ, D_MODEL), 1.0)
    mem = nrm((BATCH, N_MEM, D_MODEL), 1.0)
    norm_mix_g = gain((L, D_MODEL))
    w_in = nrm((L, D_MODEL, IN_PROJ_DIM), D_MODEL ** -0.5)
    conv_w = nrm((L, CONV_W, REC_WIDTH), CONV_W ** -0.5)
    conv_b = nrm((L, REC_WIDTH), 0.02)
    w_rg_a = nrm((L, REC_BLOCKS, REC_BLOCK_DIM, REC_BLOCK_DIM), REC_BLOCK_DIM ** -0.5)
    b_rg_a = nrm((L, REC_WIDTH), 0.02)
    w_rg_x = nrm((L, REC_BLOCKS, REC_BLOCK_DIM, REC_BLOCK_DIM), REC_BLOCK_DIM ** -0.5)
    b_rg_x = nrm((L, REC_WIDTH), 0.02)
    a_c = jax.random.uniform(next(ks), (L, REC_WIDTH), f32, 0.9, 0.999)
    a0 = a_c ** (1.0 / RG_C)
    rg_lambda = jnp.log(a0) - jnp.log1p(-a0)
    rec_norm_g = gain((L, REC_WIDTH))
    lambda_q1 = nrm((L, DA_HEAD_DIM), 0.1)
    lambda_k1 = nrm((L, DA_HEAD_DIM), 0.1)
    lambda_q2 = nrm((L, DA_HEAD_DIM), 0.1)
    lambda_k2 = nrm((L, DA_HEAD_DIM), 0.1)
    subln_g = gain((L, 2 * DA_HEAD_DIM))
    w_out = nrm((L, D_MIX, D_MODEL), D_MIX ** -0.5)
    norm_cross_g = gain((L, D_MODEL))
    norm_mem_g = gain((L, D_MODEL))
    w_cq = nrm((L, D_MODEL, D_MODEL), D_MODEL ** -0.5)
    w_ckv = nrm((L, D_MODEL, 2 * D_MODEL), D_MODEL ** -0.5)
    w_co = nrm((L, D_MODEL, D_MODEL), D_MODEL ** -0.5)
    norm_ffn_g = gain((L, D_MODEL))
    w_router = nrm((L, D_MODEL, N_EXPERTS), D_MODEL ** -0.5)
    b_router = nrm((L, N_EXPERTS), 0.01)
    w_gate_up = nrm((L, N_EXPERTS, D_MODEL, 2 * D_FF), D_MODEL ** -0.5)
    b_gate_up = nrm((L, N_EXPERTS, 2 * D_FF), 0.02)
    w_down = nrm((L, N_EXPERTS, D_FF, D_MODEL), D_FF ** -0.5)
    b_down = nrm((L, N_EXPERTS, D_MODEL), 0.02)
    norm_final_g = gain((D_MODEL,))
    return {"x": x, "mem": mem, "norm_mix_g": norm_mix_g, "w_in": w_in,
            "conv_w": conv_w, "conv_b": conv_b, "w_rg_a": w_rg_a, "b_rg_a": b_rg_a,
            "w_rg_x": w_rg_x, "b_rg_x": b_rg_x, "rg_lambda": rg_lambda, "rec_norm_g": rec_norm_g,
            "lambda_q1": lambda_q1, "lambda_k1": lambda_k1, "lambda_q2": lambda_q2, "lambda_k2": lambda_k2,
            "subln_g": subln_g, "w_out": w_out, "norm_cross_g": norm_cross_g, "norm_mem_g": norm_mem_g,
            "w_cq": w_cq, "w_ckv": w_ckv, "w_co": w_co, "norm_ffn_g": norm_ffn_g,
            "w_router": w_router, "b_router": b_router, "w_gate_up": w_gate_up, "b_gate_up": b_gate_up,
            "w_down": w_down, "b_down": b_down, "norm_final_g": norm_final_g}


def reference(x, mem, norm_mix_g, w_in, conv_w, conv_b, w_rg_a, b_rg_a, w_rg_x, b_rg_x,
              rg_lambda, rec_norm_g, lambda_q1, lambda_k1, lambda_q2, lambda_k2, subln_g, w_out,
              norm_cross_g, norm_mem_g, w_cq, w_ckv, w_co, norm_ffn_g, w_router, b_router,
              w_gate_up, b_gate_up, w_down, b_down, norm_final_g):
    h = x
    o_k = ATTN_WIDTH
    o_v = 2 * ATTN_WIDTH
    o_r = 3 * ATTN_WIDTH
    o_g = 3 * ATTN_WIDTH + REC_WIDTH
    for l in range(DEPTH):
        lambda_init = 0.8 - 0.6 * math.exp(-0.3 * l)
        n = rmsnorm(h, norm_mix_g[l])
        proj = n @ w_in[l]
        q = proj[..., :o_k]
        k = proj[..., o_k:o_v]
        v = proj[..., o_v:o_r]
        xr = proj[..., o_r:o_g]
        xg = proj[..., o_g:]
        attn_out = diff_attention(q, k, v, lambda_q1[l], lambda_k1[l], lambda_q2[l], lambda_k2[l],
                                  subln_g[l], lambda_init)
        rec_out = rg_lru_group(xr, xg, conv_w[l], conv_b[l], w_rg_a[l], b_rg_a[l], w_rg_x[l], b_rg_x[l],
                               rg_lambda[l], rec_norm_g[l])
        h = h + jnp.concatenate([attn_out, rec_out], axis=-1) @ w_out[l]
        h = h + cross_attention(rmsnorm(h, norm_cross_g[l]), rmsnorm(mem, norm_mem_g[l]),
                                w_cq[l], w_ckv[l], w_co[l])
        h = h + moe(rmsnorm(h, norm_ffn_g[l]), w_router[l], b_router[l], w_gate_up[l], b_gate_up[l],
                    w_down[l], b_down[l])
    return rmsnorm(h, norm_final_g)
```

```python
import functools
import math

import jax
import jax.numpy as jnp
from jax import lax
from jax.experimental import pallas as pl
from jax.experimental.pallas import tpu as pltpu

F32 = jnp.float32
BF16 = jnp.bfloat16
U32 = jnp.uint32
I32 = jnp.int32

DA_HEAD_DIM = 128
DA_HEADS = 4
REC_BLOCKS = 8
CONV_W = 4
RG_C = 8.0
CROSS_HEADS = 4
N_EXPERTS = 32
TOP_K = 4
SWIGLU_LIMIT = 7.0
SWIGLU_ALPHA = 1.702
EPS = 1e-6
DA_EPS = 1e-5
LAMBDA_INIT = 0.8 - 0.6 * math.exp(-0.3 * 0)
LOG2E = 1.4426950408889634

V7X_VMEM_BYTES = 64 * 1024 * 1024
VMEM_LIMIT = V7X_VMEM_BYTES - 8 * 1024 * 1024

GROUP_ROWS = 1024
SUB_ROWS = 256
FF_TILE = 256


def _cparams(sem):
    return pltpu.CompilerParams(dimension_semantics=sem, vmem_limit_bytes=VMEM_LIMIT)


def _rms(x, g, eps):
    return x * lax.rsqrt(jnp.mean(x * x, axis=-1, keepdims=True) + eps) * g


def _norm_matmul_kernel(x_ref, g_ref, w_ref, cs_ref, o_ref, xn_ref):
    @pl.when(pl.program_id(1) == 0)
    def _():
        xn_ref[...] = _rms(x_ref[...], g_ref[...], EPS).astype(BF16)

    acc = jnp.dot(xn_ref[...], w_ref[...], preferred_element_type=F32)
    o_ref[...] = (acc * cs_ref[...]).astype(o_ref.dtype)


def _norm_matmul(x, g, w, col_scale, out_dtype, tm, tn):
    m, k = x.shape
    n = w.shape[1]
    tm, tn = min(tm, m), min(tn, n)
    return pl.pallas_call(
        _norm_matmul_kernel,
        grid=(m // tm, n // tn),
        in_specs=[pl.BlockSpec((tm, k), lambda i, j: (i, 0)),
                  pl.BlockSpec((1, k), lambda i, j: (0, 0)),
                  pl.BlockSpec((k, tn), lambda i, j: (0, j)),
                  pl.BlockSpec((1, tn), lambda i, j: (0, j))],
        out_specs=pl.BlockSpec((tm, tn), lambda i, j: (i, j)),
        out_shape=jax.ShapeDtypeStruct((m, n), out_dtype),
        scratch_shapes=[pltpu.VMEM((tm, k), BF16)],
        compiler_params=_cparams(("parallel", "arbitrary")),
        name="norm_matmul",
    )(x, g.reshape(1, k), w, col_scale)


def _matmul_res_kernel(a0_ref, a1_ref, w0_ref, w1_ref, r_ref, o_ref):
    acc = jnp.dot(a0_ref[...], w0_ref[...], preferred_element_type=F32)
    acc += jnp.dot(a1_ref[...], w1_ref[...], preferred_element_type=F32)
    o_ref[...] = r_ref[...] + acc


def _matmul_res(a0, c0, a1, c1, w, res, tm, tn):
    m, n = res.shape
    kh = w.shape[0] // 2
    tm, tn = min(tm, m), min(tn, n)
    return pl.pallas_call(
        _matmul_res_kernel,
        grid=(m // tm, n // tn),
        in_specs=[pl.BlockSpec((tm, kh), lambda i, j: (i, c0)),
                  pl.BlockSpec((tm, kh), lambda i, j: (i, c1)),
                  pl.BlockSpec((kh, tn), lambda i, j: (0, j)),
                  pl.BlockSpec((kh, tn), lambda i, j: (1, j)),
                  pl.BlockSpec((tm, tn), lambda i, j: (i, j))],
        out_specs=pl.BlockSpec((tm, tn), lambda i, j: (i, j)),
        out_shape=jax.ShapeDtypeStruct((m, n), F32),
        compiler_params=_cparams(("parallel", "parallel")),
        name="matmul_res",
    )(a0, a1, w, w, res)


def _diff_attn_kernel(qi_ref, ki_ref, q_ref, k_ref, v_ref, lam_ref, g_ref, o_ref,
                      m_ref, l_ref, acc_ref, *, tq, tk):
    p = pl.program_id(1)
    qi = qi_ref[p]
    ki = ki_ref[p]
    d = DA_HEAD_DIM
    c = (d ** -0.5) * LOG2E

    @pl.when(ki == 0)
    def _():
        m_ref[...] = jnp.full(m_ref.shape, -jnp.inf, F32)
        l_ref[...] = jnp.zeros(l_ref.shape, F32)
        acc_ref[...] = jnp.zeros(acc_ref.shape, F32)

    def step(masked):
        v = v_ref[...]
        for i in range(2):
            q = q_ref[:, i * d:(i + 1) * d]
            k = k_ref[:, i * d:(i + 1) * d]
            s = lax.dot_general(q, k, (((1,), (1,)), ((), ())), preferred_element_type=F32) * c
            if masked:
                row = qi * tq + lax.broadcasted_iota(I32, (tq, tk), 0)
                col = ki * tk + lax.broadcasted_iota(I32, (tq, tk), 1)
                s = jnp.where(col <= row, s, -jnp.inf)
            m_old = m_ref[i][:, :1]
            m_new = jnp.maximum(m_old, jnp.max(s, axis=-1, keepdims=True))
            alpha = jnp.exp2(m_old - m_new)
            pr = jnp.exp2(s - m_new)
            l_ref[i] = jnp.broadcast_to(alpha * l_ref[i][:, :1] + jnp.sum(pr, axis=-1, keepdims=True),
                                        l_ref.shape[1:])
            m_ref[i] = jnp.broadcast_to(m_new, m_ref.shape[1:])
            acc_ref[i] = alpha * acc_ref[i] + jnp.dot(pr.astype(BF16), v, preferred_element_type=F32)

    needs_mask = (ki + 1) * tk - 1 > qi * tq

    @pl.when(needs_mask)
    def _():
        step(True)

    @pl.when(jnp.logical_not(needs_mask))
    def _():
        step(False)

    @pl.when((ki + 1) * tk == (qi + 1) * tq)
    def _():
        lp = lam_ref[...]
        lam = (jnp.exp(jnp.sum(lp[0:1] * lp[1:2], axis=-1, keepdims=True))
               - jnp.exp(jnp.sum(lp[2:3] * lp[3:4], axis=-1, keepdims=True)) + LAMBDA_INIT)
        o = acc_ref[0] / l_ref[0][:, :1] - lam * (acc_ref[1] / l_ref[1][:, :1])
        o = _rms(o, g_ref[...], DA_EPS) * (1.0 - LAMBDA_INIT)
        o_ref[...] = o.astype(o_ref.dtype)


def _diff_attn(qkv, lam_params, subln_g, tq, tk):
    s = qkv.shape[0]
    dv = 2 * DA_HEAD_DIM
    tq, tk = min(tq, s), min(tk, s)
    nq = s // tq
    pairs = [(a, b) for a in range(nq) for b in range(((a + 1) * tq) // tk)]
    qi = jnp.asarray([a for a, _ in pairs], I32)
    ki = jnp.asarray([b for _, b in pairs], I32)
    grid_spec = pltpu.PrefetchScalarGridSpec(
        num_scalar_prefetch=2,
        grid=(DA_HEADS, len(pairs)),
        in_specs=[pl.BlockSpec((tq, dv), lambda h, p, qi, ki: (qi[p], h)),
                  pl.BlockSpec((tk, dv), lambda h, p, qi, ki: (ki[p], DA_HEADS + h)),
                  pl.BlockSpec((tk, dv), lambda h, p, qi, ki: (ki[p], 2 * DA_HEADS + h)),
                  pl.BlockSpec((4, DA_HEAD_DIM), lambda h, p, qi, ki: (0, 0)),
                  pl.BlockSpec((1, dv), lambda h, p, qi, ki: (0, 0))],
        out_specs=pl.BlockSpec((tq, dv), lambda h, p, qi, ki: (qi[p], h)),
        scratch_shapes=[pltpu.VMEM((2, tq, 128), F32),
                        pltpu.VMEM((2, tq, 128), F32),
                        pltpu.VMEM((2, tq, dv), F32)],
    )
    return pl.pallas_call(
        functools.partial(_diff_attn_kernel, tq=tq, tk=tk),
        grid_spec=grid_spec,
        out_shape=jax.ShapeDtypeStruct((s, DA_HEADS * dv), BF16),
        compiler_params=_cparams(("parallel", "arbitrary")),
        name="diff_attn",
    )(qi, ki, qkv, qkv, qkv, lam_params, subln_g.reshape(1, dv))


def _shift_rows(x, d, fill):
    rolled = pltpu.roll(x, d, axis=0)
    row = lax.broadcasted_iota(I32, x.shape, 0)
    return jnp.where(row < d, fill, rolled)


def _rglru_kernel(xr_ref, xg_ref, cw_ref, cb_ref, wa_ref, ba_ref, wx_ref, bx_ref, lam_ref, g_ref,
                  o_ref, hist_ref, h_ref, *, t):
    @pl.when(pl.program_id(0) == 0)
    def _():
        hist_ref[...] = jnp.zeros(hist_ref.shape, F32)
        h_ref[...] = jnp.zeros(h_ref.shape, F32)

    xr = xr_ref[...]
    c = xr.shape[1]
    bd = c // REC_BLOCKS
    row = lax.broadcasted_iota(I32, xr.shape, 0)
    hist = hist_ref[...]
    cw = cw_ref[...]
    xc = xr * cw[CONV_W - 1:CONV_W] + cb_ref[...]
    for dlt in range(1, CONV_W):
        prev = pltpu.roll(xr, dlt, axis=0)
        hrow = jnp.concatenate([hist[8 - dlt:8]] + [hist[0:8 - dlt]], axis=0)
        hfull = jnp.tile(hrow, (xr.shape[0] // 8, 1))
        sh = jnp.where(row < dlt, hfull, prev)
        xc = xc + sh * cw[CONV_W - 1 - dlt:CONV_W - dlt]
    hist_ref[...] = xr[t - 8:t]

    xcb = xc.astype(BF16)
    ra, ri = [], []
    for n in range(REC_BLOCKS):
        xb = xcb[:, n * bd:(n + 1) * bd]
        ra.append(jnp.dot(xb, wa_ref[n], preferred_element_type=F32))
        ri.append(jnp.dot(xb, wx_ref[n], preferred_element_type=F32))
    r = jax.nn.sigmoid(jnp.concatenate(ra, axis=1) + ba_ref[...])
    ig = jax.nn.sigmoid(jnp.concatenate(ri, axis=1) + bx_ref[...])
    log_a = (-RG_C * jax.nn.softplus(-lam_ref[...])) * r
    a = jnp.exp(log_a)
    b = jnp.sqrt(-jnp.tanh(log_a) * (a * a + 1.0)) * (ig * xc)

    dlt = 1
    while dlt < t:
        a_sh = _shift_rows(a, dlt, 1.0)
        b_sh = _shift_rows(b, dlt, 0.0)
        b = a * b_sh + b
        a = a * a_sh
        dlt *= 2
    h = a * h_ref[0:1] + b
    h_ref[...] = jnp.broadcast_to(h[t - 1:t], h_ref.shape)

    y = h * jax.nn.gelu(xg_ref[...])
    o_ref[...] = _rms(y, g_ref[...], EPS).astype(o_ref.dtype)


def _rglru(rg, conv_w, conv_b, w_a, b_a, w_x, b_x, rg_lambda, rec_norm_g, t):
    s = rg.shape[0]
    c = rg.shape[1] // 2
    t = min(t, s)
    bd = c // REC_BLOCKS
    vec = lambda: pl.BlockSpec((1, c), lambda i: (0, 0))
    return pl.pallas_call(
        functools.partial(_rglru_kernel, t=t),
        grid=(s // t,),
        in_specs=[pl.BlockSpec((t, c), lambda i: (i, 0)),
                  pl.BlockSpec((t, c), lambda i: (i, 1)),
                  pl.BlockSpec((CONV_W, c), lambda i: (0, 0)),
                  vec(),
                  pl.BlockSpec((REC_BLOCKS, bd, bd), lambda i: (0, 0, 0)),
                  vec(),
                  pl.BlockSpec((REC_BLOCKS, bd, bd), lambda i: (0, 0, 0)),
                  vec(), vec(), vec()],
        out_specs=pl.BlockSpec((t, c), lambda i: (i, 0)),
        out_shape=jax.ShapeDtypeStruct((s, c), BF16),
        scratch_shapes=[pltpu.VMEM((8, c), F32), pltpu.VMEM((8, c), F32)],
        compiler_params=_cparams(("arbitrary",)),
        name="rglru",
    )(rg, rg, conv_w, conv_b.reshape(1, c), w_a.astype(BF16), b_a.reshape(1, c),
      w_x.astype(BF16), b_x.reshape(1, c), rg_lambda.reshape(1, c), rec_norm_g.reshape(1, c))


def _cross_attn_kernel(h_ref, g_ref, wq_ref, kv_ref, o_ref):
    d = h_ref.shape[1]
    hd = d // CROSS_HEADS
    c = (hd ** -0.5) * LOG2E
    n = _rms(h_ref[...], g_ref[...], EPS).astype(BF16)
    q = jnp.dot(n, wq_ref[...], preferred_element_type=F32).astype(BF16)
    for hh in range(CROSS_HEADS):
        qh = q[:, hh * hd:(hh + 1) * hd]
        kh = kv_ref[:, hh * hd:(hh + 1) * hd]
        vh = kv_ref[:, d + hh * hd:d + (hh + 1) * hd]
        s = lax.dot_general(qh, kh, (((1,), (1,)), ((), ())), preferred_element_type=F32) * c
        pr = jnp.exp2(s - jnp.max(s, axis=-1, keepdims=True))
        l = jnp.sum(pr, axis=-1, keepdims=True)
        o = jnp.dot(pr.astype(BF16), vh, preferred_element_type=F32) / l
        o_ref[:, hh * hd:(hh + 1) * hd] = o.astype(o_ref.dtype)


def _cross_attn(h, g, wq, kv, tm):
    s, d = h.shape
    nm = kv.shape[0]
    tm = min(tm, s)
    return pl.pallas_call(
        _cross_attn_kernel,
        grid=(s // tm,),
        in_specs=[pl.BlockSpec((tm, d), lambda i: (i, 0)),
                  pl.BlockSpec((1, d), lambda i: (0, 0)),
                  pl.BlockSpec((d, d), lambda i: (0, 0)),
                  pl.BlockSpec((nm, 2 * d), lambda i: (0, 0))],
        out_specs=pl.BlockSpec((tm, d), lambda i: (i, 0)),
        out_shape=jax.ShapeDtypeStruct((s, d), BF16),
        compiler_params=_cparams(("parallel",)),
        name="cross_attn",
    )(h, g.reshape(1, d), wq, kv)


def _router_kernel(h_ref, g_ref, wrt_ref, br_ref, tri_ref,
                   xp_ref, idx_ref, gate_ref, rank_ref, cnt_ref, carry_ref):
    @pl.when(pl.program_id(0) == 0)
    def _():
        carry_ref[...] = jnp.zeros(carry_ref.shape, F32)

    tm, d = h_ref.shape
    half = d // 2
    hn = _rms(h_ref[...], g_ref[...], EPS)
    bits = pltpu.bitcast(hn.astype(BF16).astype(F32), U32)
    xp_ref[...] = (bits[:, :half] >> 16) | (bits[:, half:] & jnp.uint32(0xFFFF0000))

    logits = lax.dot_general(wrt_ref[...], hn, (((1,), (1,)), ((), ())),
                             precision=lax.Precision.HIGHEST,
                             preferred_element_type=F32) + br_ref[...]
    e_iota = lax.broadcasted_iota(I32, logits.shape, 0)
    vals, idxs, hits = [], [], []
    l = logits
    for _ in range(TOP_K):
        mx = jnp.max(l, axis=0, keepdims=True)
        ik = jnp.min(jnp.where(l == mx, e_iota, N_EXPERTS), axis=0, keepdims=True)
        hit = e_iota == ik
        vals.append(mx)
        idxs.append(ik)
        hits.append(hit)
        l = jnp.where(hit, -jnp.inf, l)
    ex = [jnp.exp(v - vals[0]) for v in vals]
    den = ex[0] + ex[1] + ex[2] + ex[3]
    gate_ref[...] = jnp.concatenate([e / den for e in ex], axis=0)
    idx_ref[...] = jnp.concatenate(idxs, axis=0)

    onehot = jnp.zeros(logits.shape, F32)
    for hit in hits:
        onehot = onehot + jnp.where(hit, 1.0, 0.0)
    before = jnp.dot(onehot.astype(BF16), tri_ref[...], preferred_element_type=F32)
    before = before + carry_ref[:, :1]
    ranks = [jnp.sum(jnp.where(hit, before, 0.0), axis=0, keepdims=True) for hit in hits]
    rank_ref[...] = jnp.concatenate(ranks, axis=0).astype(I32)
    carry = carry_ref[...] + jnp.sum(onehot, axis=1, keepdims=True)
    carry_ref[...] = carry
    cnt_ref[...] = carry


def _router(h, g, w_router, b_router, tm):
    s, d = h.shape
    tm = min(tm, s)
    tri = jnp.triu(jnp.ones((tm, tm), BF16), 1)
    kt = lambda: pl.BlockSpec((TOP_K, tm), lambda i: (0, i))
    return pl.pallas_call(
        _router_kernel,
        grid=(s // tm,),
        in_specs=[pl.BlockSpec((tm, d), lambda i: (i, 0)),
                  pl.BlockSpec((1, d), lambda i: (0, 0)),
                  pl.BlockSpec((N_EXPERTS, d), lambda i: (0, 0)),
                  pl.BlockSpec((N_EXPERTS, 1), lambda i: (0, 0)),
                  pl.BlockSpec((tm, tm), lambda i: (0, 0))],
        out_specs=[pl.BlockSpec((tm, d // 2), lambda i: (i, 0)), kt(), kt(), kt(),
                   pl.BlockSpec((N_EXPERTS, 128), lambda i: (0, 0))],
        out_shape=[jax.ShapeDtypeStruct((s, d // 2), U32),
                   jax.ShapeDtypeStruct((TOP_K, s), I32),
                   jax.ShapeDtypeStruct((TOP_K, s), F32),
                   jax.ShapeDtypeStruct((TOP_K, s), I32),
                   jax.ShapeDtypeStruct((N_EXPERTS, 128), F32)],
        scratch_shapes=[pltpu.VMEM((N_EXPERTS, 128), F32)],
        compiler_params=_cparams(("arbitrary",)),
        name="router",
    )(h, g.reshape(1, d), w_router.T, b_router.reshape(N_EXPERTS, 1), tri)


def _dispatch_kernel(dest_ref, x_ref, xs_ref, sem):
    tm = x_ref.shape[0]

    def row_copy(r, k):
        return pltpu.make_async_copy(x_ref.at[pl.ds(r, 1)], xs_ref.at[pl.ds(dest_ref[k, r], 1)], sem)

    def start(r, carry):
        for k in range(TOP_K):
            row_copy(r, k).start()
        return carry

    def wait(r, carry):
        for k in range(TOP_K):
            row_copy(r, k).wait()
        return carry

    lax.fori_loop(0, tm, start, 0)
    lax.fori_loop(0, tm, wait, 0)


def _dispatch(xp, dest, m_pad, tm):
    s, w = xp.shape
    tm = min(tm, s)
    return pl.pallas_call(
        _dispatch_kernel,
        grid=(s // tm,),
        in_specs=[pl.BlockSpec((TOP_K, tm), lambda i: (0, i), memory_space=pltpu.SMEM),
                  pl.BlockSpec((tm, w), lambda i: (i, 0))],
        out_specs=pl.BlockSpec(memory_space=pl.ANY),
        out_shape=jax.ShapeDtypeStruct((m_pad, w), U32),
        scratch_shapes=[pltpu.SemaphoreType.DMA],
        compiler_params=_cparams(("arbitrary",)),
        name="dispatch",
    )(dest, xp)


def _experts_kernel(ge_ref, gx_ref, nsub_ref, nrow_ref,
                    x_ref, wg_ref, wu_ref, wd_ref, bg_ref, bu_ref, bd_ref, o_ref, xb_ref):
    g = pl.program_id(0)
    f = pl.program_id(1)
    ns = nsub_ref[g]
    half = x_ref.shape[1]
    n_sub_max = GROUP_ROWS // SUB_ROWS

    @pl.when(f == 0)
    def _():
        w = x_ref[...]
        valid = lax.broadcasted_iota(I32, w.shape, 0) < nrow_ref[g]
        lo = pltpu.bitcast(w << 16, F32)
        hi = pltpu.bitcast(w & jnp.uint32(0xFFFF0000), F32)
        xb_ref[:, :half] = jnp.where(valid, lo, 0.0).astype(BF16)
        xb_ref[:, half:] = jnp.where(valid, hi, 0.0).astype(BF16)

        def zero(j, carry):
            r0 = pl.multiple_of(j * SUB_ROWS, SUB_ROWS)
            o_ref[pl.ds(r0, SUB_ROWS), :] = jnp.zeros((SUB_ROWS, o_ref.shape[1]), F32)
            return carry

        lax.fori_loop(ns, n_sub_max, zero, 0)

    wg = wg_ref[0].astype(BF16)
    wu = wu_ref[0].astype(BF16)
    wd = wd_ref[0].astype(BF16)
    bg = bg_ref[0]
    bu = bu_ref[0]

    def sub_block(j, carry):
        r0 = pl.multiple_of(j * SUB_ROWS, SUB_ROWS)
        xj = xb_ref[pl.ds(r0, SUB_ROWS), :]
        gate = jnp.dot(xj, wg, preferred_element_type=F32) + bg
        up = jnp.dot(xj, wu, preferred_element_type=F32) + bu
        gate = jnp.minimum(gate, SWIGLU_LIMIT)
        up = jnp.clip(up, -SWIGLU_LIMIT, SWIGLU_LIMIT)
        act = (up + 1.0) * gate * jax.nn.sigmoid(SWIGLU_ALPHA * gate)
        y = jnp.dot(act.astype(BF16), wd, preferred_element_type=F32)

        @pl.when(f == 0)
        def _():
            o_ref[pl.ds(r0, SUB_ROWS), :] = y + bd_ref[0]

        @pl.when(f > 0)
        def _():
            o_ref[pl.ds(r0, SUB_ROWS), :] += y

        return carry

    lax.fori_loop(0, ns, sub_block, 0)


def _experts(xs, w_gu, b_gu, w_down, b_down, g_e, g_x, g_nsub, g_nrow):
    m_pad, half = xs.shape
    d = 2 * half
    e, _, ff2 = w_gu.shape
    ff = ff2 // 2
    nf = ff // FF_TILE
    ng = m_pad // GROUP_ROWS
    def ft(g, f, ns):
        return jnp.where(ns[g] > 0, f, nf - 1)

    grid_spec = pltpu.PrefetchScalarGridSpec(
        num_scalar_prefetch=4,
        grid=(ng, nf),
        in_specs=[pl.BlockSpec((GROUP_ROWS, half), lambda g, f, ge, gx, ns, nr: (gx[g], 0)),
                  pl.BlockSpec((1, d, FF_TILE), lambda g, f, ge, gx, ns, nr: (ge[g], 0, ft(g, f, ns))),
                  pl.BlockSpec((1, d, FF_TILE), lambda g, f, ge, gx, ns, nr: (ge[g], 0, nf + ft(g, f, ns))),
                  pl.BlockSpec((1, FF_TILE, d), lambda g, f, ge, gx, ns, nr: (ge[g], ft(g, f, ns), 0)),
                  pl.BlockSpec((1, 1, FF_TILE), lambda g, f, ge, gx, ns, nr: (ge[g], 0, ft(g, f, ns))),
                  pl.BlockSpec((1, 1, FF_TILE), lambda g, f, ge, gx, ns, nr: (ge[g], 0, nf + ft(g, f, ns))),
                  pl.BlockSpec((1, 1, d), lambda g, f, ge, gx, ns, nr: (ge[g], 0, 0))],
        out_specs=pl.BlockSpec((GROUP_ROWS, d), lambda g, f, ge, gx, ns, nr: (g, 0)),
        scratch_shapes=[pltpu.VMEM((GROUP_ROWS, d), BF16)],
    )
    return pl.pallas_call(
        _experts_kernel,
        grid_spec=grid_spec,
        out_shape=jax.ShapeDtypeStruct((m_pad, d), F32),
        compiler_params=_cparams(("arbitrary", "arbitrary")),
        name="experts",
    )(g_e, g_x, g_nsub, g_nrow, xs, w_gu, w_gu, w_down,
      b_gu.reshape(e, 1, ff2), b_gu.reshape(e, 1, ff2), b_down.reshape(e, 1, d))


def _combine_kernel(dest_ref, gate_ref, h_ref, g_ref, y_ref, o_ref, buf_ref, sem):
    tm = h_ref.shape[0]

    def row_copy(r, k):
        return pltpu.make_async_copy(y_ref.at[pl.ds(dest_ref[k, r], 1)], buf_ref.at[k, pl.ds(r, 1)], sem)

    def start(r, carry):
        for k in range(TOP_K):
            row_copy(r, k).start()
        return carry

    def wait(r, carry):
        for k in range(TOP_K):
            row_copy(r, k).wait()
        return carry

    lax.fori_loop(0, tm, start, 0)
    lax.fori_loop(0, tm, wait, 0)
    acc = h_ref[...]
    gates = gate_ref[...]
    for k in range(TOP_K):
        acc = acc + buf_ref[k] * gates[:, k:k + 1]
    o_ref[...] = _rms(acc, g_ref[...], EPS)


def _combine(y, dest, gates_t, h, g, tm):
    s, d = h.shape
    tm = min(tm, s)
    return pl.pallas_call(
        _combine_kernel,
        grid=(s // tm,),
        in_specs=[pl.BlockSpec((TOP_K, tm), lambda i: (0, i), memory_space=pltpu.SMEM),
                  pl.BlockSpec((tm, TOP_K), lambda i: (i, 0)),
                  pl.BlockSpec((tm, d), lambda i: (i, 0)),
                  pl.BlockSpec((1, d), lambda i: (0, 0)),
                  pl.BlockSpec(memory_space=pl.ANY)],
        out_specs=pl.BlockSpec((tm, d), lambda i: (i, 0)),
        out_shape=jax.ShapeDtypeStruct((s, d), F32),
        scratch_shapes=[pltpu.VMEM((TOP_K, tm, d), F32), pltpu.SemaphoreType.DMA],
        compiler_params=_cparams(("arbitrary",)),
        name="combine",
    )(dest, gates_t, h, g.reshape(1, d), y)


def _routing_tables(idx, rank, counts, n_groups):
    seg_groups = (counts + GROUP_ROWS - 1) // GROUP_ROWS
    grp_end = jnp.cumsum(seg_groups)
    grp_start = grp_end - seg_groups
    dest = (grp_start * GROUP_ROWS)[idx] + rank
    total = grp_end[-1]
    gid = jnp.arange(n_groups, dtype=I32)
    used = gid < total
    gclamp = jnp.minimum(gid, total - 1)
    g_e = jnp.minimum(jnp.searchsorted(grp_end, gclamp, side="right"), N_EXPERTS - 1).astype(I32)
    rows = jnp.clip(counts[g_e] - (gclamp - grp_start[g_e]) * GROUP_ROWS, 0, GROUP_ROWS)
    g_nrow = jnp.where(used, rows, 0).astype(I32)
    g_nsub = (g_nrow + SUB_ROWS - 1) // SUB_ROWS
    return dest.astype(I32), g_e, gclamp.astype(I32), g_nsub.astype(I32), g_nrow


def kernel(x, mem, norm_mix_g, w_in, conv_w, conv_b, w_rg_a, b_rg_a, w_rg_x, b_rg_x, rg_lambda, rec_norm_g,
           lambda_q1, lambda_k1, lambda_q2, lambda_k2, subln_g, w_out, norm_cross_g, norm_mem_g, w_cq, w_ckv,
           w_co, norm_ffn_g, w_router, b_router, w_gate_up, b_gate_up, w_down, b_down, norm_final_g):
    b, s, d = x.shape
    assert b == 1 and w_in.shape[0] == 1
    aw = DA_HEADS * 2 * DA_HEAD_DIM
    h0 = x.reshape(s, d)

    w_in_b = w_in[0].astype(BF16)
    ones = lambda n: jnp.ones((1, n), F32)
    qkv = _norm_matmul(h0, norm_mix_g[0], w_in_b[:, :3 * aw], ones(3 * aw), BF16, 1024, 1024)
    rg = _norm_matmul(h0, norm_mix_g[0], w_in_b[:, 3 * aw:], ones(w_in_b.shape[1] - 3 * aw), F32, 1024, 1024)
    lam_params = jnp.stack([lambda_q1[0], lambda_k1[0], lambda_q2[0], lambda_k2[0]])
    attn = _diff_attn(qkv, lam_params, subln_g[0], 1024, 1024)
    rec = _rglru(rg, conv_w[0], conv_b[0], w_rg_a[0], b_rg_a[0], w_rg_x[0], b_rg_x[0],
                 rg_lambda[0], rec_norm_g[0], 256)
    h1 = _matmul_res(attn, 0, rec, 0, w_out[0].astype(BF16), h0, 1024, 1024)

    nm = mem.shape[1]
    kv = _norm_matmul(mem.reshape(nm, d), norm_mem_g[0], w_ckv[0].astype(BF16), ones(2 * d), BF16, 256, 1024)
    o = _cross_attn(h1, norm_cross_g[0], w_cq[0].astype(BF16), kv, 512)
    h2 = _matmul_res(o, 0, o, 1, w_co[0].astype(BF16), h1, 1024, 1024)

    xp, idx, gates, rank, cnt = _router(h2, norm_ffn_g[0], w_router[0], b_router[0], 512)
    n_groups = (s * TOP_K) // GROUP_ROWS + N_EXPERTS
    dest, g_e, g_x, g_nsub, g_nrow = _routing_tables(idx, rank, cnt[:, 0].astype(I32), n_groups)
    xs = _dispatch(xp, dest, n_groups * GROUP_ROWS, 256)
    y = _experts(xs, w_gate_up[0], b_gate_up[0], w_down[0], b_down[0], g_e, g_x, g_nsub, g_nrow)
    out = _combine(y, dest, gates.T, h2, norm_final_g, 128)
    return out.reshape(b, s, d)
```

```python
import functools
import math

import jax
import jax.numpy as jnp
from jax import lax
from jax.experimental import pallas as pl
from jax.experimental.pallas import tpu as pltpu

F32 = jnp.float32
BF16 = jnp.bfloat16
U32 = jnp.uint32
I32 = jnp.int32

DA_HEAD_DIM = 128
DA_HEADS = 4
REC_BLOCKS = 8
CONV_W = 4
RG_C = 8.0
CROSS_HEADS = 4
N_EXPERTS = 32
TOP_K = 4
SWIGLU_LIMIT = 7.0
SWIGLU_ALPHA = 1.702
EPS = 1e-6
DA_EPS = 1e-5
LAMBDA_INIT = 0.8 - 0.6 * math.exp(-0.3 * 0)
LOG2E = 1.4426950408889634

V7X_VMEM_BYTES = 64 * 1024 * 1024
VMEM_LIMIT = V7X_VMEM_BYTES - 8 * 1024 * 1024

GROUP_ROWS = 1024
SUB_ROWS = 256
FF_TILE = 256


def _cparams(sem):
    return pltpu.CompilerParams(dimension_semantics=sem, vmem_limit_bytes=VMEM_LIMIT)


def _rms(x, g, eps):
    return x * lax.rsqrt(jnp.mean(x * x, axis=-1, keepdims=True) + eps) * g


def _norm_matmul_kernel(x_ref, g_ref, w_ref, cs_ref, o_ref, xn_ref):
    @pl.when(pl.program_id(1) == 0)
    def _():
        xn_ref[...] = _rms(x_ref[...], g_ref[...], EPS).astype(BF16)

    acc = jnp.dot(xn_ref[...], w_ref[...], preferred_element_type=F32)
    o_ref[...] = (acc * cs_ref[...]).astype(o_ref.dtype)


def _norm_matmul(x, g, w, col_scale, out_dtype, tm, tn):
    m, k = x.shape
    n = w.shape[1]
    tm, tn = min(tm, m), min(tn, n)
    return pl.pallas_call(
        _norm_matmul_kernel,
        grid=(m // tm, n // tn),
        in_specs=[pl.BlockSpec((tm, k), lambda i, j: (i, 0)),
                  pl.BlockSpec((1, k), lambda i, j: (0, 0)),
                  pl.BlockSpec((k, tn), lambda i, j: (0, j)),
                  pl.BlockSpec((1, tn), lambda i, j: (0, j))],
        out_specs=pl.BlockSpec((tm, tn), lambda i, j: (i, j)),
        out_shape=jax.ShapeDtypeStruct((m, n), out_dtype),
        scratch_shapes=[pltpu.VMEM((tm, k), BF16)],
        compiler_params=_cparams(("parallel", "arbitrary")),
        name="norm_matmul",
    )(x, g.reshape(1, k), w, col_scale)


def _matmul_res_kernel(a0_ref, a1_ref, w0_ref, w1_ref, r_ref, o_ref):
    acc = jnp.dot(a0_ref[...], w0_ref[...], preferred_element_type=F32)
    acc += jnp.dot(a1_ref[...], w1_ref[...], preferred_element_type=F32)
    o_ref[...] = r_ref[...] + acc


def _matmul_res(a0, c0, a1, c1, w, res, tm, tn):
    m, n = res.shape
    kh = w.shape[0] // 2
    tm, tn = min(tm, m), min(tn, n)
    return pl.pallas_call(
        _matmul_res_kernel,
        grid=(m // tm, n // tn),
        in_specs=[pl.BlockSpec((tm, kh), lambda i, j: (i, c0)),
                  pl.BlockSpec((tm, kh), lambda i, j: (i, c1)),
                  pl.BlockSpec((kh, tn), lambda i, j: (0, j)),
                  pl.BlockSpec((kh, tn), lambda i, j: (1, j)),
                  pl.BlockSpec((tm, tn), lambda i, j: (i, j))],
        out_specs=pl.BlockSpec((tm, tn), lambda i, j: (i, j)),
        out_shape=jax.ShapeDtypeStruct((m, n), F32),
        compiler_params=_cparams(("parallel", "parallel")),
        name="matmul_res",
    )(a0, a1, w, w, res)


def _diff_attn_kernel(qi_ref, ki_ref, q_ref, k_ref, v_ref, lam_ref, g_ref, o_ref,
                      m_ref, l_ref, acc_ref, *, t, n_split):
    p = pl.program_id(1)
    qi = qi_ref[p]
    ki = ki_ref[p]
    d = DA_HEAD_DIM
    rs = t // n_split

    @pl.when(ki == 0)
    def _():
        m_ref[...] = jnp.full(m_ref.shape, -jnp.inf, F32)
        l_ref[...] = jnp.zeros(l_ref.shape, F32)
        acc_ref[...] = jnp.zeros(acc_ref.shape, F32)

    def step(diag):
        for h in range(n_split):
            r0 = h * rs
            kw = r0 + rs if diag else t
            v = v_ref[0:kw, :]
            for i in range(2):
                q = q_ref[r0:r0 + rs, i * d:(i + 1) * d]
                k = k_ref[0:kw, i * d:(i + 1) * d]
                s = lax.dot_general(q, k, (((1,), (1,)), ((), ())), preferred_element_type=F32)
                if diag:
                    row = r0 + lax.broadcasted_iota(I32, (rs, kw), 0)
                    col = lax.broadcasted_iota(I32, (rs, kw), 1)
                    s = jnp.where(col <= row, s, -jnp.inf)
                m_old = m_ref[i, r0:r0 + rs, :1]
                m_new = jnp.maximum(m_old, jnp.max(s, axis=-1, keepdims=True))
                alpha = jnp.exp2(m_old - m_new)
                pr = jnp.exp2(s - m_new)
                l_ref[i, r0:r0 + rs, :] = jnp.broadcast_to(
                    alpha * l_ref[i, r0:r0 + rs, :1] + jnp.sum(pr, axis=-1, keepdims=True), (rs, 128))
                m_ref[i, r0:r0 + rs, :] = jnp.broadcast_to(m_new, (rs, 128))
                acc_ref[i, r0:r0 + rs, :] = alpha * acc_ref[i, r0:r0 + rs, :] + jnp.dot(
                    pr.astype(BF16), v, preferred_element_type=F32)

    @pl.when(ki == qi)
    def _():
        step(True)

    @pl.when(ki < qi)
    def _():
        step(False)

    @pl.when(ki == qi)
    def _():
        lp = lam_ref[...]
        lam = (jnp.exp(jnp.sum(lp[0:1] * lp[1:2], axis=-1, keepdims=True))
               - jnp.exp(jnp.sum(lp[2:3] * lp[3:4], axis=-1, keepdims=True)) + LAMBDA_INIT)
        o = acc_ref[0] / l_ref[0][:, :1] - lam * (acc_ref[1] / l_ref[1][:, :1])
        o = _rms(o, g_ref[...], DA_EPS) * (1.0 - LAMBDA_INIT)
        o_ref[...] = o.astype(o_ref.dtype)


def _diff_attn(qkv, lam_params, subln_g, t, n_split):
    s = qkv.shape[0]
    dv = 2 * DA_HEAD_DIM
    t = min(t, s)
    n_split = min(n_split, t // 128)
    assert s % t == 0 and t % n_split == 0
    nq = s // t
    pairs = [(a, b) for a in range(nq) for b in range(a + 1)]
    qi = jnp.asarray([a for a, _ in pairs], I32)
    ki = jnp.asarray([b for _, b in pairs], I32)
    grid_spec = pltpu.PrefetchScalarGridSpec(
        num_scalar_prefetch=2,
        grid=(DA_HEADS, len(pairs)),
        in_specs=[pl.BlockSpec((t, dv), lambda h, p, qi, ki: (qi[p], h)),
                  pl.BlockSpec((t, dv), lambda h, p, qi, ki: (ki[p], DA_HEADS + h)),
                  pl.BlockSpec((t, dv), lambda h, p, qi, ki: (ki[p], 2 * DA_HEADS + h)),
                  pl.BlockSpec((4, DA_HEAD_DIM), lambda h, p, qi, ki: (0, 0)),
                  pl.BlockSpec((1, dv), lambda h, p, qi, ki: (0, 0))],
        out_specs=pl.BlockSpec((t, dv), lambda h, p, qi, ki: (qi[p], h)),
        scratch_shapes=[pltpu.VMEM((2, t, 128), F32),
                        pltpu.VMEM((2, t, 128), F32),
                        pltpu.VMEM((2, t, dv), F32)],
    )
    return pl.pallas_call(
        functools.partial(_diff_attn_kernel, t=t, n_split=n_split),
        grid_spec=grid_spec,
        out_shape=jax.ShapeDtypeStruct((s, DA_HEADS * dv), BF16),
        compiler_params=_cparams(("parallel", "arbitrary")),
        name="diff_attn",
    )(qi, ki, qkv, qkv, qkv, lam_params, subln_g.reshape(1, dv))


def _shift_rows(x, d, fill):
    rolled = pltpu.roll(x, d, axis=0)
    row = lax.broadcasted_iota(I32, x.shape, 0)
    return jnp.where(row < d, fill, rolled)


def _rglru_kernel(xr_ref, xg_ref, cw_ref, cb_ref, wa_ref, ba_ref, wx_ref, bx_ref, lam_ref, g_ref,
                  o_ref, hist_ref, h_ref, *, t):
    @pl.when(pl.program_id(0) == 0)
    def _():
        hist_ref[...] = jnp.zeros(hist_ref.shape, F32)
        h_ref[...] = jnp.zeros(h_ref.shape, F32)

    xr = xr_ref[...]
    c = xr.shape[1]
    bd = c // REC_BLOCKS
    row = lax.broadcasted_iota(I32, xr.shape, 0)
    hist = hist_ref[...]
    cw = cw_ref[...]
    xc = xr * cw[CONV_W - 1:CONV_W] + cb_ref[...]
    for dlt in range(1, CONV_W):
        prev = pltpu.roll(xr, dlt, axis=0)
        hrow = jnp.concatenate([hist[8 - dlt:8]] + [hist[0:8 - dlt]], axis=0)
        hfull = jnp.tile(hrow, (xr.shape[0] // 8, 1))
        sh = jnp.where(row < dlt, hfull, prev)
        xc = xc + sh * cw[CONV_W - 1 - dlt:CONV_W - dlt]
    hist_ref[...] = xr[t - 8:t]

    xcb = xc.astype(BF16)
    ra, ri = [], []
    for n in range(REC_BLOCKS):
        xb = xcb[:, n * bd:(n + 1) * bd]
        ra.append(jnp.dot(xb, wa_ref[n], preferred_element_type=F32))
        ri.append(jnp.dot(xb, wx_ref[n], preferred_element_type=F32))
    r = jax.nn.sigmoid(jnp.concatenate(ra, axis=1) + ba_ref[...])
    ig = jax.nn.sigmoid(jnp.concatenate(ri, axis=1) + bx_ref[...])
    log_a = (-RG_C * jax.nn.softplus(-lam_ref[...])) * r
    a = jnp.exp(log_a)
    b = jnp.sqrt(-jnp.tanh(log_a) * (a * a + 1.0)) * (ig * xc)

    dlt = 1
    while dlt < t:
        a_sh = _shift_rows(a, dlt, 1.0)
        b_sh = _shift_rows(b, dlt, 0.0)
        b = a * b_sh + b
        a = a * a_sh
        dlt *= 2
    h = a * h_ref[0:1] + b
    h_ref[...] = jnp.broadcast_to(h[t - 1:t], h_ref.shape)

    y = h * jax.nn.gelu(xg_ref[...])
    o_ref[...] = _rms(y, g_ref[...], EPS).astype(o_ref.dtype)


def _rglru(rg, conv_w, conv_b, w_a, b_a, w_x, b_x, rg_lambda, rec_norm_g, t):
    s = rg.shape[0]
    c = rg.shape[1] // 2
    t = min(t, s)
    bd = c // REC_BLOCKS
    vec = lambda: pl.BlockSpec((1, c), lambda i: (0, 0))
    return pl.pallas_call(
        functools.partial(_rglru_kernel, t=t),
        grid=(s // t,),
        in_specs=[pl.BlockSpec((t, c), lambda i: (i, 0)),
                  pl.BlockSpec((t, c), lambda i: (i, 1)),
                  pl.BlockSpec((CONV_W, c), lambda i: (0, 0)),
                  vec(),
                  pl.BlockSpec((REC_BLOCKS, bd, bd), lambda i: (0, 0, 0)),
                  vec(),
                  pl.BlockSpec((REC_BLOCKS, bd, bd), lambda i: (0, 0, 0)),
                  vec(), vec(), vec()],
        out_specs=pl.BlockSpec((t, c), lambda i: (i, 0)),
        out_shape=jax.ShapeDtypeStruct((s, c), BF16),
        scratch_shapes=[pltpu.VMEM((8, c), F32), pltpu.VMEM((8, c), F32)],
        compiler_params=_cparams(("arbitrary",)),
        name="rglru",
    )(rg, rg, conv_w, conv_b.reshape(1, c), w_a.astype(BF16), b_a.reshape(1, c),
      w_x.astype(BF16), b_x.reshape(1, c), rg_lambda.reshape(1, c), rec_norm_g.reshape(1, c))


def _cross_attn_kernel(h_ref, g_ref, wq_ref, kv_ref, o_ref):
    d = h_ref.shape[1]
    hd = d // CROSS_HEADS
    c = (hd ** -0.5) * LOG2E
    n = _rms(h_ref[...], g_ref[...], EPS).astype(BF16)
    q = jnp.dot(n, wq_ref[...], preferred_element_type=F32).astype(BF16)
    for hh in range(CROSS_HEADS):
        qh = q[:, hh * hd:(hh + 1) * hd]
        kh = kv_ref[:, hh * hd:(hh + 1) * hd]
        vh = kv_ref[:, d + hh * hd:d + (hh + 1) * hd]
        s = lax.dot_general(qh, kh, (((1,), (1,)), ((), ())), preferred_element_type=F32) * c
        pr = jnp.exp2(s - jnp.max(s, axis=-1, keepdims=True))
        l = jnp.sum(pr, axis=-1, keepdims=True)
        o = jnp.dot(pr.astype(BF16), vh, preferred_element_type=F32) / l
        o_ref[:, hh * hd:(hh + 1) * hd] = o.astype(o_ref.dtype)


def _cross_attn(h, g, wq, kv, tm):
    s, d = h.shape
    nm = kv.shape[0]
    tm = min(tm, s)
    return pl.pallas_call(
        _cross_attn_kernel,
        grid=(s // tm,),
        in_specs=[pl.BlockSpec((tm, d), lambda i: (i, 0)),
                  pl.BlockSpec((1, d), lambda i: (0, 0)),
                  pl.BlockSpec((d, d), lambda i: (0, 0)),
                  pl.BlockSpec((nm, 2 * d), lambda i: (0, 0))],
        out_specs=pl.BlockSpec((tm, d), lambda i: (i, 0)),
        out_shape=jax.ShapeDtypeStruct((s, d), BF16),
        compiler_params=_cparams(("parallel",)),
        name="cross_attn",
    )(h, g.reshape(1, d), wq, kv)


def _router_kernel(h_ref, g_ref, wrt_ref, br_ref, tri_ref,
                   xp_ref, idx_ref, gate_ref, rank_ref, cnt_ref, carry_ref):
    @pl.when(pl.program_id(0) == 0)
    def _():
        carry_ref[...] = jnp.zeros(carry_ref.shape, F32)

    tm, d = h_ref.shape
    half = d // 2
    hn = _rms(h_ref[...], g_ref[...], EPS)
    bits = pltpu.bitcast(hn.astype(BF16).astype(F32), U32)
    xp_ref[...] = (bits[:, :half] >> 16) | (bits[:, half:] & jnp.uint32(0xFFFF0000))

    logits = lax.dot_general(wrt_ref[...], hn, (((1,), (1,)), ((), ())),
                             precision=lax.Precision.HIGHEST,
                             preferred_element_type=F32) + br_ref[...]
    e_iota = lax.broadcasted_iota(I32, logits.shape, 0)
    vals, idxs, hits = [], [], []
    l = logits
    for _ in range(TOP_K):
        mx = jnp.max(l, axis=0, keepdims=True)
        ik = jnp.min(jnp.where(l == mx, e_iota, N_EXPERTS), axis=0, keepdims=True)
        hit = e_iota == ik
        vals.append(mx)
        idxs.append(ik)
        hits.append(hit)
        l = jnp.where(hit, -jnp.inf, l)
    ex = [jnp.exp(v - vals[0]) for v in vals]
    den = ex[0] + ex[1] + ex[2] + ex[3]
    gate_ref[...] = jnp.concatenate([e / den for e in ex], axis=0)
    idx_ref[...] = jnp.concatenate(idxs, axis=0)

    onehot = jnp.zeros(logits.shape, F32)
    for hit in hits:
        onehot = onehot + jnp.where(hit, 1.0, 0.0)
    before = jnp.dot(onehot.astype(BF16), tri_ref[...], preferred_element_type=F32)
    before = before + carry_ref[:, :1]
    ranks = [jnp.sum(jnp.where(hit, before, 0.0), axis=0, keepdims=True) for hit in hits]
    rank_ref[...] = jnp.concatenate(ranks, axis=0).astype(I32)
    carry = carry_ref[...] + jnp.sum(onehot, axis=1, keepdims=True)
    carry_ref[...] = carry
    cnt_ref[...] = carry


def _router(h, g, w_router, b_router, tm):
    s, d = h.shape
    tm = min(tm, s)
    tri = jnp.triu(jnp.ones((tm, tm), BF16), 1)
    kt = lambda: pl.BlockSpec((TOP_K, tm), lambda i: (0, i))
    return pl.pallas_call(
        _router_kernel,
        grid=(s // tm,),
        in_specs=[pl.BlockSpec((tm, d), lambda i: (i, 0)),
                  pl.BlockSpec((1, d), lambda i: (0, 0)),
                  pl.BlockSpec((N_EXPERTS, d), lambda i: (0, 0)),
                  pl.BlockSpec((N_EXPERTS, 1), lambda i: (0, 0)),
                  pl.BlockSpec((tm, tm), lambda i: (0, 0))],
        out_specs=[pl.BlockSpec((tm, d // 2), lambda i: (i, 0)), kt(), kt(), kt(),
                   pl.BlockSpec((N_EXPERTS, 128), lambda i: (0, 0))],
        out_shape=[jax.ShapeDtypeStruct((s, d // 2), U32),
                   jax.ShapeDtypeStruct((TOP_K, s), I32),
                   jax.ShapeDtypeStruct((TOP_K, s), F32),
                   jax.ShapeDtypeStruct((TOP_K, s), I32),
                   jax.ShapeDtypeStruct((N_EXPERTS, 128), F32)],
        scratch_shapes=[pltpu.VMEM((N_EXPERTS, 128), F32)],
        compiler_params=_cparams(("arbitrary",)),
        name="router",
    )(h, g.reshape(1, d), w_router.T, b_router.reshape(N_EXPERTS, 1), tri)


def _dispatch_kernel(dest_ref, x_ref, xs_ref, sem):
    tm = x_ref.shape[0]

    def row_copy(r, k):
        return pltpu.make_async_copy(x_ref.at[pl.ds(r, 1)], xs_ref.at[pl.ds(dest_ref[k, r], 1)], sem)

    def start(r, carry):
        for k in range(TOP_K):
            row_copy(r, k).start()
        return carry

    def wait(r, carry):
        for k in range(TOP_K):
            row_copy(r, k).wait()
        return carry

    lax.fori_loop(0, tm, start, 0)
    lax.fori_loop(0, tm, wait, 0)


def _dispatch(xp, dest, m_pad, tm):
    s, w = xp.shape
    tm = min(tm, s)
    return pl.pallas_call(
        _dispatch_kernel,
        grid=(s // tm,),
        in_specs=[pl.BlockSpec((TOP_K, tm), lambda i: (0, i), memory_space=pltpu.SMEM),
                  pl.BlockSpec((tm, w), lambda i: (i, 0))],
        out_specs=pl.BlockSpec(memory_space=pl.ANY),
        out_shape=jax.ShapeDtypeStruct((m_pad, w), U32),
        scratch_shapes=[pltpu.SemaphoreType.DMA],
        compiler_params=_cparams(("arbitrary",)),
        name="dispatch",
    )(dest, xp)


def _experts_kernel(ge_ref, gx_ref, nsub_ref, nrow_ref,
                    x_ref, wg_ref, wu_ref, wd_ref, bg_ref, bu_ref, bd_ref, o_ref, xb_ref):
    g = pl.program_id(0)
    f = pl.program_id(1)
    ns = nsub_ref[g]
    half = x_ref.shape[1]
    n_sub_max = GROUP_ROWS // SUB_ROWS

    @pl.when(f == 0)
    def _():
        w = x_ref[...]
        valid = lax.broadcasted_iota(I32, w.shape, 0) < nrow_ref[g]
        lo = pltpu.bitcast(w << 16, F32)
        hi = pltpu.bitcast(w & jnp.uint32(0xFFFF0000), F32)
        xb_ref[:, :half] = jnp.where(valid, lo, 0.0).astype(BF16)
        xb_ref[:, half:] = jnp.where(valid, hi, 0.0).astype(BF16)
        o_ref[...] = jnp.broadcast_to(bd_ref[0], o_ref.shape)

    def compute(rows):
        xj = xb_ref[0:rows, :]
        gate = jnp.dot(xj, wg_ref[0].astype(BF16), preferred_element_type=F32) + bg_ref[0]
        up = jnp.dot(xj, wu_ref[0].astype(BF16), preferred_element_type=F32) + bu_ref[0]
        gate = jnp.minimum(gate, SWIGLU_LIMIT)
        up = jnp.clip(up, -SWIGLU_LIMIT, SWIGLU_LIMIT)
        act = (up + 1.0) * gate * jax.nn.sigmoid(SWIGLU_ALPHA * gate)
        o_ref[0:rows, :] += jnp.dot(act.astype(BF16), wd_ref[0].astype(BF16), preferred_element_type=F32)

    for n in range(1, n_sub_max + 1):
        @pl.when(ns == n)
        def _(n=n):
            compute(n * SUB_ROWS)


def _experts(xs, w_gu, b_gu, w_down, b_down, g_e, g_x, g_nsub, g_nrow):
    m_pad, half = xs.shape
    d = 2 * half
    e, _, ff2 = w_gu.shape
    ff = ff2 // 2
    nf = ff // FF_TILE
    ng = m_pad // GROUP_ROWS
    def ft(g, f, ns):
        return jnp.where(ns[g] > 0, f, nf - 1)

    grid_spec = pltpu.PrefetchScalarGridSpec(
        num_scalar_prefetch=4,
        grid=(ng, nf),
        in_specs=[pl.BlockSpec((GROUP_ROWS, half), lambda g, f, ge, gx, ns, nr: (gx[g], 0)),
                  pl.BlockSpec((1, d, FF_TILE), lambda g, f, ge, gx, ns, nr: (ge[g], 0, ft(g, f, ns))),
                  pl.BlockSpec((1, d, FF_TILE), lambda g, f, ge, gx, ns, nr: (ge[g], 0, nf + ft(g, f, ns))),
                  pl.BlockSpec((1, FF_TILE, d), lambda g, f, ge, gx, ns, nr: (ge[g], ft(g, f, ns), 0)),
                  pl.BlockSpec((1, 1, FF_TILE), lambda g, f, ge, gx, ns, nr: (ge[g], 0, ft(g, f, ns))),
                  pl.BlockSpec((1, 1, FF_TILE), lambda g, f, ge, gx, ns, nr: (ge[g], 0, nf + ft(g, f, ns))),
                  pl.BlockSpec((1, 1, d), lambda g, f, ge, gx, ns, nr: (ge[g], 0, 0))],
        out_specs=pl.BlockSpec((GROUP_ROWS, d), lambda g, f, ge, gx, ns, nr: (g, 0)),
        scratch_shapes=[pltpu.VMEM((GROUP_ROWS, d), BF16)],
    )
    return pl.pallas_call(
        _experts_kernel,
        grid_spec=grid_spec,
        out_shape=jax.ShapeDtypeStruct((m_pad, d), F32),
        compiler_params=_cparams(("arbitrary", "arbitrary")),
        name="experts",
    )(g_e, g_x, g_nsub, g_nrow, xs, w_gu, w_gu, w_down,
      b_gu.reshape(e, 1, ff2), b_gu.reshape(e, 1, ff2), b_down.reshape(e, 1, d))


def _combine_kernel(dest_ref, gate_ref, h_ref, g_ref, y_ref, o_ref, buf_ref, sem):
    tm = h_ref.shape[0]

    def row_copy(r, k):
        return pltpu.make_async_copy(y_ref.at[pl.ds(dest_ref[k, r], 1)], buf_ref.at[k, pl.ds(r, 1)], sem)

    def start(r, carry):
        for k in range(TOP_K):
            row_copy(r, k).start()
        return carry

    def wait(r, carry):
        for k in range(TOP_K):
            row_copy(r, k).wait()
        return carry

    lax.fori_loop(0, tm, start, 0)
    lax.fori_loop(0, tm, wait, 0)
    acc = h_ref[...]
    gates = gate_ref[...]
    for k in range(TOP_K):
        acc = acc + buf_ref[k] * gates[:, k:k + 1]
    o_ref[...] = _rms(acc, g_ref[...], EPS)


def _combine(y, dest, gates_t, h, g, tm):
    s, d = h.shape
    tm = min(tm, s)
    return pl.pallas_call(
        _combine_kernel,
        grid=(s // tm,),
        in_specs=[pl.BlockSpec((TOP_K, tm), lambda i: (0, i), memory_space=pltpu.SMEM),
                  pl.BlockSpec((tm, TOP_K), lambda i: (i, 0)),
                  pl.BlockSpec((tm, d), lambda i: (i, 0)),
                  pl.BlockSpec((1, d), lambda i: (0, 0)),
                  pl.BlockSpec(memory_space=pl.ANY)],
        out_specs=pl.BlockSpec((tm, d), lambda i: (i, 0)),
        out_shape=jax.ShapeDtypeStruct((s, d), F32),
        scratch_shapes=[pltpu.VMEM((TOP_K, tm, d), F32), pltpu.SemaphoreType.DMA],
        compiler_params=_cparams(("arbitrary",)),
        name="combine",
    )(dest, gates_t, h, g.reshape(1, d), y)


def _routing_tables(idx, rank, counts, n_groups):
    seg_groups = (counts + GROUP_ROWS - 1) // GROUP_ROWS
    grp_end = jnp.cumsum(seg_groups)
    grp_start = grp_end - seg_groups
    row_start = grp_start * GROUP_ROWS
    hit = idx[..., None] == jnp.arange(N_EXPERTS, dtype=I32)
    dest = jnp.sum(jnp.where(hit, row_start, 0), axis=-1) + rank
    total = grp_end[-1]
    gid = jnp.arange(n_groups, dtype=I32)
    used = gid < total
    gclamp = jnp.minimum(gid, total - 1)
    g_e = jnp.minimum(jnp.searchsorted(grp_end, gclamp, side="right"), N_EXPERTS - 1).astype(I32)
    rows = jnp.clip(counts[g_e] - (gclamp - grp_start[g_e]) * GROUP_ROWS, 0, GROUP_ROWS)
    g_nrow = jnp.where(used, rows, 0).astype(I32)
    g_nsub = (g_nrow + SUB_ROWS - 1) // SUB_ROWS
    return dest.astype(I32), g_e, gclamp.astype(I32), g_nsub.astype(I32), g_nrow


def kernel(x, mem, norm_mix_g, w_in, conv_w, conv_b, w_rg_a, b_rg_a, w_rg_x, b_rg_x, rg_lambda, rec_norm_g,
           lambda_q1, lambda_k1, lambda_q2, lambda_k2, subln_g, w_out, norm_cross_g, norm_mem_g, w_cq, w_ckv,
           w_co, norm_ffn_g, w_router, b_router, w_gate_up, b_gate_up, w_down, b_down, norm_final_g):
    b, s, d = x.shape
    assert b == 1 and w_in.shape[0] == 1
    aw = DA_HEADS * 2 * DA_HEAD_DIM
    h0 = x.reshape(s, d)

    w_in_b = w_in[0].astype(BF16)
    ones = lambda n: jnp.ones((1, n), F32)
    q_scale = jnp.concatenate([jnp.full((1, aw), (DA_HEAD_DIM ** -0.5) * LOG2E, F32), ones(2 * aw)], axis=1)
    qkv = _norm_matmul(h0, norm_mix_g[0], w_in_b[:, :3 * aw], q_scale, BF16, 1024, 1024)
    rg = _norm_matmul(h0, norm_mix_g[0], w_in_b[:, 3 * aw:], ones(w_in_b.shape[1] - 3 * aw), F32, 1024, 1024)
    lam_params = jnp.stack([lambda_q1[0], lambda_k1[0], lambda_q2[0], lambda_k2[0]])
    attn = _diff_attn(qkv, lam_params, subln_g[0], 2048, 8)
    rec = _rglru(rg, conv_w[0], conv_b[0], w_rg_a[0], b_rg_a[0], w_rg_x[0], b_rg_x[0],
                 rg_lambda[0], rec_norm_g[0], 256)
    h1 = _matmul_res(attn, 0, rec, 0, w_out[0].astype(BF16), h0, 1024, 1024)

    nm = mem.shape[1]
    kv = _norm_matmul(mem.reshape(nm, d), norm_mem_g[0], w_ckv[0].astype(BF16), ones(2 * d), BF16, 256, 1024)
    o = _cross_attn(h1, norm_cross_g[0], w_cq[0].astype(BF16), kv, 512)
    h2 = _matmul_res(o, 0, o, 1, w_co[0].astype(BF16), h1, 1024, 1024)

    xp, idx, gates, rank, cnt = _router(h2, norm_ffn_g[0], w_router[0], b_router[0], 512)
    n_groups = (s * TOP_K) // GROUP_ROWS + N_EXPERTS
    dest, g_e, g_x, g_nsub, g_nrow = _routing_tables(idx, rank, cnt[:, 0].astype(I32), n_groups)
    xs = _dispatch(xp, dest, n_groups * GROUP_ROWS, 256)
    y = _experts(xs, w_gate_up[0], b_gate_up[0], w_down[0], b_down[0], g_e, g_x, g_nsub, g_nrow)
    out = _combine(y, dest, gates.T, h2, norm_final_g, 128)
    return out.reshape(b, s, d)
```

```python
import functools
import math

import jax
import jax.numpy as jnp
from jax import lax
from jax.experimental import pallas as pl
from jax.experimental.pallas import tpu as pltpu

F32 = jnp.float32
BF16 = jnp.bfloat16
U32 = jnp.uint32
I32 = jnp.int32

DA_HEAD_DIM = 128
DA_HEADS = 4
REC_BLOCKS = 8
CONV_W = 4
RG_C = 8.0
CROSS_HEADS = 4
N_EXPERTS = 32
TOP_K = 4
SWIGLU_LIMIT = 7.0
SWIGLU_ALPHA = 1.702
EPS = 1e-6
DA_EPS = 1e-5
LAMBDA_INIT = 0.8 - 0.6 * math.exp(-0.3 * 0)
LOG2E = 1.4426950408889634

V7X_VMEM_BYTES = 64 * 1024 * 1024
VMEM_LIMIT = V7X_VMEM_BYTES - 8 * 1024 * 1024

GROUP_ROWS = 1024
SUB_ROWS = 256
FF_TILE = 256


def _cparams(sem):
    return pltpu.CompilerParams(dimension_semantics=sem, vmem_limit_bytes=VMEM_LIMIT)


def _rms(x, g, eps):
    return x * lax.rsqrt(jnp.mean(x * x, axis=-1, keepdims=True) + eps) * g


def _norm_matmul_kernel(x_ref, g_ref, w_ref, cs_ref, o_ref, xn_ref):
    @pl.when(pl.program_id(1) == 0)
    def _():
        xn_ref[...] = _rms(x_ref[...], g_ref[...], EPS).astype(BF16)

    acc = jnp.dot(xn_ref[...], w_ref[...], preferred_element_type=F32)
    o_ref[...] = (acc * cs_ref[...]).astype(o_ref.dtype)


def _norm_matmul(x, g, w, col_scale, out_dtype, tm, tn):
    m, k = x.shape
    n = w.shape[1]
    tm, tn = min(tm, m), min(tn, n)
    return pl.pallas_call(
        _norm_matmul_kernel,
        grid=(m // tm, n // tn),
        in_specs=[pl.BlockSpec((tm, k), lambda i, j: (i, 0)),
                  pl.BlockSpec((1, k), lambda i, j: (0, 0)),
                  pl.BlockSpec((k, tn), lambda i, j: (0, j)),
                  pl.BlockSpec((1, tn), lambda i, j: (0, j))],
        out_specs=pl.BlockSpec((tm, tn), lambda i, j: (i, j)),
        out_shape=jax.ShapeDtypeStruct((m, n), out_dtype),
        scratch_shapes=[pltpu.VMEM((tm, k), BF16)],
        compiler_params=_cparams(("parallel", "arbitrary")),
        name="norm_matmul",
    )(x, g.reshape(1, k), w, col_scale)


def _matmul_res_kernel(a0_ref, a1_ref, w0_ref, w1_ref, r_ref, o_ref):
    acc = jnp.dot(a0_ref[...], w0_ref[...], preferred_element_type=F32)
    acc += jnp.dot(a1_ref[...], w1_ref[...], preferred_element_type=F32)
    o_ref[...] = r_ref[...] + acc


def _matmul_res(a0, c0, a1, c1, w, res, tm, tn):
    m, n = res.shape
    kh = w.shape[0] // 2
    tm, tn = min(tm, m), min(tn, n)
    return pl.pallas_call(
        _matmul_res_kernel,
        grid=(m // tm, n // tn),
        in_specs=[pl.BlockSpec((tm, kh), lambda i, j: (i, c0)),
                  pl.BlockSpec((tm, kh), lambda i, j: (i, c1)),
                  pl.BlockSpec((kh, tn), lambda i, j: (0, j)),
                  pl.BlockSpec((kh, tn), lambda i, j: (1, j)),
                  pl.BlockSpec((tm, tn), lambda i, j: (i, j))],
        out_specs=pl.BlockSpec((tm, tn), lambda i, j: (i, j)),
        out_shape=jax.ShapeDtypeStruct((m, n), F32),
        compiler_params=_cparams(("parallel", "parallel")),
        name="matmul_res",
    )(a0, a1, w, w, res)


def _diff_attn_kernel(qi_ref, ki_ref, q_ref, k_ref, v_ref, lam_ref, g_ref, o_ref,
                      m_ref, l_ref, acc_ref, *, t, n_split):
    p = pl.program_id(1)
    qi = qi_ref[p]
    ki = ki_ref[p]
    d = DA_HEAD_DIM
    rs = t // n_split

    @pl.when(ki == 0)
    def _():
        m_ref[...] = jnp.full(m_ref.shape, -jnp.inf, F32)
        l_ref[...] = jnp.zeros(l_ref.shape, F32)
        acc_ref[...] = jnp.zeros(acc_ref.shape, F32)

    def step(diag):
        for h in range(n_split):
            r0 = h * rs
            kw = r0 + rs if diag else t
            v = v_ref[0:kw, :]
            for i in range(2):
                q = q_ref[r0:r0 + rs, i * d:(i + 1) * d]
                k = k_ref[0:kw, i * d:(i + 1) * d]
                s = lax.dot_general(q, k, (((1,), (1,)), ((), ())), preferred_element_type=F32)
                if diag:
                    row = r0 + lax.broadcasted_iota(I32, (rs, kw), 0)
                    col = lax.broadcasted_iota(I32, (rs, kw), 1)
                    s = jnp.where(col <= row, s, -jnp.inf)
                m_old = m_ref[i, r0:r0 + rs, :1]
                m_new = jnp.maximum(m_old, jnp.max(s, axis=-1, keepdims=True))
                alpha = jnp.exp2(m_old - m_new)
                pr = jnp.exp2(s - m_new)
                l_ref[i, r0:r0 + rs, :] = jnp.broadcast_to(
                    alpha * l_ref[i, r0:r0 + rs, :1] + jnp.sum(pr, axis=-1, keepdims=True), (rs, 128))
                m_ref[i, r0:r0 + rs, :] = jnp.broadcast_to(m_new, (rs, 128))
                acc_ref[i, r0:r0 + rs, :] = alpha * acc_ref[i, r0:r0 + rs, :] + jnp.dot(
                    pr.astype(BF16), v, preferred_element_type=F32)

    @pl.when(ki == qi)
    def _():
        step(True)

    @pl.when(ki < qi)
    def _():
        step(False)

    @pl.when(ki == qi)
    def _():
        lp = lam_ref[...]
        lam = (jnp.exp(jnp.sum(lp[0:1] * lp[1:2], axis=-1, keepdims=True))
               - jnp.exp(jnp.sum(lp[2:3] * lp[3:4], axis=-1, keepdims=True)) + LAMBDA_INIT)
        o = acc_ref[0] / l_ref[0][:, :1] - lam * (acc_ref[1] / l_ref[1][:, :1])
        o = _rms(o, g_ref[...], DA_EPS) * (1.0 - LAMBDA_INIT)
        o_ref[...] = o.astype(o_ref.dtype)


def _diff_attn(qkv, lam_params, subln_g, t, n_split):
    s = qkv.shape[0]
    dv = 2 * DA_HEAD_DIM
    t = min(t, s)
    n_split = min(n_split, t // 128)
    assert s % t == 0 and t % n_split == 0
    nq = s // t
    pairs = [(a, b) for a in range(nq) for b in range(a + 1)]
    qi = jnp.asarray([a for a, _ in pairs], I32)
    ki = jnp.asarray([b for _, b in pairs], I32)
    grid_spec = pltpu.PrefetchScalarGridSpec(
        num_scalar_prefetch=2,
        grid=(DA_HEADS, len(pairs)),
        in_specs=[pl.BlockSpec((t, dv), lambda h, p, qi, ki: (qi[p], h)),
                  pl.BlockSpec((t, dv), lambda h, p, qi, ki: (ki[p], DA_HEADS + h)),
                  pl.BlockSpec((t, dv), lambda h, p, qi, ki: (ki[p], 2 * DA_HEADS + h)),
                  pl.BlockSpec((4, DA_HEAD_DIM), lambda h, p, qi, ki: (0, 0)),
                  pl.BlockSpec((1, dv), lambda h, p, qi, ki: (0, 0))],
        out_specs=pl.BlockSpec((t, dv), lambda h, p, qi, ki: (qi[p], h)),
        scratch_shapes=[pltpu.VMEM((2, t, 128), F32),
                        pltpu.VMEM((2, t, 128), F32),
                        pltpu.VMEM((2, t, dv), F32)],
    )
    return pl.pallas_call(
        functools.partial(_diff_attn_kernel, t=t, n_split=n_split),
        grid_spec=grid_spec,
        out_shape=jax.ShapeDtypeStruct((s, DA_HEADS * dv), BF16),
        compiler_params=_cparams(("parallel", "arbitrary")),
        name="diff_attn",
    )(qi, ki, qkv, qkv, qkv, lam_params, subln_g.reshape(1, dv))


def _shift_rows(x, d, fill):
    rolled = pltpu.roll(x, d, axis=0)
    row = lax.broadcasted_iota(I32, x.shape, 0)
    return jnp.where(row < d, fill, rolled)


def _rglru_kernel(xr_ref, xg_ref, cw_ref, cb_ref, wa_ref, ba_ref, wx_ref, bx_ref, lam_ref, g_ref,
                  o_ref, hist_ref, h_ref, *, t):
    @pl.when(pl.program_id(0) == 0)
    def _():
        hist_ref[...] = jnp.zeros(hist_ref.shape, F32)
        h_ref[...] = jnp.zeros(h_ref.shape, F32)

    xr = xr_ref[...]
    c = xr.shape[1]
    bd = c // REC_BLOCKS
    row = lax.broadcasted_iota(I32, xr.shape, 0)
    hist = hist_ref[...]
    cw = cw_ref[...]
    xc = xr * cw[CONV_W - 1:CONV_W] + cb_ref[...]
    for dlt in range(1, CONV_W):
        prev = pltpu.roll(xr, dlt, axis=0)
        hrow = jnp.concatenate([hist[8 - dlt:8]] + [hist[0:8 - dlt]], axis=0)
        hfull = jnp.tile(hrow, (xr.shape[0] // 8, 1))
        sh = jnp.where(row < dlt, hfull, prev)
        xc = xc + sh * cw[CONV_W - 1 - dlt:CONV_W - dlt]
    hist_ref[...] = xr[t - 8:t]

    xcb = xc.astype(BF16)
    ra, ri = [], []
    for n in range(REC_BLOCKS):
        xb = xcb[:, n * bd:(n + 1) * bd]
        ra.append(jnp.dot(xb, wa_ref[n], preferred_element_type=F32))
        ri.append(jnp.dot(xb, wx_ref[n], preferred_element_type=F32))
    r = jax.nn.sigmoid(jnp.concatenate(ra, axis=1) + ba_ref[...])
    ig = jax.nn.sigmoid(jnp.concatenate(ri, axis=1) + bx_ref[...])
    log_a = (-RG_C * jax.nn.softplus(-lam_ref[...])) * r
    a = jnp.exp(log_a)
    b = jnp.sqrt(-jnp.tanh(log_a) * (a * a + 1.0)) * (ig * xc)

    dlt = 1
    while dlt < t:
        a_sh = _shift_rows(a, dlt, 1.0)
        b_sh = _shift_rows(b, dlt, 0.0)
        b = a * b_sh + b
        a = a * a_sh
        dlt *= 2
    h = a * h_ref[0:1] + b
    h_ref[...] = jnp.broadcast_to(h[t - 1:t], h_ref.shape)

    y = h * jax.nn.gelu(xg_ref[...])
    o_ref[...] = _rms(y, g_ref[...], EPS).astype(o_ref.dtype)


def _rglru(rg, conv_w, conv_b, w_a, b_a, w_x, b_x, rg_lambda, rec_norm_g, t):
    s = rg.shape[0]
    c = rg.shape[1] // 2
    t = min(t, s)
    bd = c // REC_BLOCKS
    vec = lambda: pl.BlockSpec((1, c), lambda i: (0, 0))
    return pl.pallas_call(
        functools.partial(_rglru_kernel, t=t),
        grid=(s // t,),
        in_specs=[pl.BlockSpec((t, c), lambda i: (i, 0)),
                  pl.BlockSpec((t, c), lambda i: (i, 1)),
                  pl.BlockSpec((CONV_W, c), lambda i: (0, 0)),
                  vec(),
                  pl.BlockSpec((REC_BLOCKS, bd, bd), lambda i: (0, 0, 0)),
                  vec(),
                  pl.BlockSpec((REC_BLOCKS, bd, bd), lambda i: (0, 0, 0)),
                  vec(), vec(), vec()],
        out_specs=pl.BlockSpec((t, c), lambda i: (i, 0)),
        out_shape=jax.ShapeDtypeStruct((s, c), BF16),
        scratch_shapes=[pltpu.VMEM((8, c), F32), pltpu.VMEM((8, c), F32)],
        compiler_params=_cparams(("arbitrary",)),
        name="rglru",
    )(rg, rg, conv_w, conv_b.reshape(1, c), w_a.astype(BF16), b_a.reshape(1, c),
      w_x.astype(BF16), b_x.reshape(1, c), rg_lambda.reshape(1, c), rec_norm_g.reshape(1, c))


def _cross_attn_kernel(h_ref, g_ref, wq_ref, kv_ref, o_ref):
    d = h_ref.shape[1]
    hd = d // CROSS_HEADS
    c = (hd ** -0.5) * LOG2E
    n = _rms(h_ref[...], g_ref[...], EPS).astype(BF16)
    q = jnp.dot(n, wq_ref[...], preferred_element_type=F32).astype(BF16)
    for hh in range(CROSS_HEADS):
        qh = q[:, hh * hd:(hh + 1) * hd]
        kh = kv_ref[:, hh * hd:(hh + 1) * hd]
        vh = kv_ref[:, d + hh * hd:d + (hh + 1) * hd]
        s = lax.dot_general(qh, kh, (((1,), (1,)), ((), ())), preferred_element_type=F32) * c
        pr = jnp.exp2(s - jnp.max(s, axis=-1, keepdims=True))
        l = jnp.sum(pr, axis=-1, keepdims=True)
        o = jnp.dot(pr.astype(BF16), vh, preferred_element_type=F32) / l
        o_ref[:, hh * hd:(hh + 1) * hd] = o.astype(o_ref.dtype)


def _cross_attn(h, g, wq, kv, tm):
    s, d = h.shape
    nm = kv.shape[0]
    tm = min(tm, s)
    return pl.pallas_call(
        _cross_attn_kernel,
        grid=(s // tm,),
        in_specs=[pl.BlockSpec((tm, d), lambda i: (i, 0)),
                  pl.BlockSpec((1, d), lambda i: (0, 0)),
                  pl.BlockSpec((d, d), lambda i: (0, 0)),
                  pl.BlockSpec((nm, 2 * d), lambda i: (0, 0))],
        out_specs=pl.BlockSpec((tm, d), lambda i: (i, 0)),
        out_shape=jax.ShapeDtypeStruct((s, d), BF16),
        compiler_params=_cparams(("parallel",)),
        name="cross_attn",
    )(h, g.reshape(1, d), wq, kv)


def _router_kernel(h_ref, g_ref, wrt_ref, br_ref, tri_ref,
                   xp_ref, idx_ref, gate_ref, rank_ref, cnt_ref, carry_ref):
    @pl.when(pl.program_id(0) == 0)
    def _():
        carry_ref[...] = jnp.zeros(carry_ref.shape, F32)

    tm, d = h_ref.shape
    half = d // 2
    hn = _rms(h_ref[...], g_ref[...], EPS)
    bits = pltpu.bitcast(hn.astype(BF16).astype(F32), U32)
    xp_ref[...] = (bits[:, :half] >> 16) | (bits[:, half:] & jnp.uint32(0xFFFF0000))

    logits = lax.dot_general(wrt_ref[...], hn, (((1,), (1,)), ((), ())),
                             precision=lax.Precision.HIGHEST,
                             preferred_element_type=F32) + br_ref[...]
    e_iota = lax.broadcasted_iota(I32, logits.shape, 0)
    vals, idxs, hits = [], [], []
    l = logits
    for _ in range(TOP_K):
        mx = jnp.max(l, axis=0, keepdims=True)
        ik = jnp.min(jnp.where(l == mx, e_iota, N_EXPERTS), axis=0, keepdims=True)
        hit = e_iota == ik
        vals.append(mx)
        idxs.append(ik)
        hits.append(hit)
        l = jnp.where(hit, -jnp.inf, l)
    ex = [jnp.exp(v - vals[0]) for v in vals]
    den = ex[0] + ex[1] + ex[2] + ex[3]
    gate_ref[...] = jnp.concatenate([e / den for e in ex], axis=0)
    idx_ref[...] = jnp.concatenate(idxs, axis=0)

    onehot = jnp.zeros(logits.shape, F32)
    for hit in hits:
        onehot = onehot + jnp.where(hit, 1.0, 0.0)
    before = jnp.dot(onehot.astype(BF16), tri_ref[...], preferred_element_type=F32)
    before = before + carry_ref[:, :1]
    ranks = [jnp.sum(jnp.where(hit, before, 0.0), axis=0, keepdims=True) for hit in hits]
    rank_ref[...] = jnp.concatenate(ranks, axis=0).astype(I32)
    carry = carry_ref[...] + jnp.sum(onehot, axis=1, keepdims=True)
    carry_ref[...] = carry
    cnt_ref[...] = carry


def _router(h, g, w_router, b_router, tm):
    s, d = h.shape
    tm = min(tm, s)
    tri = jnp.triu(jnp.ones((tm, tm), BF16), 1)
    kt = lambda: pl.BlockSpec((TOP_K, tm), lambda i: (0, i))
    return pl.pallas_call(
        _router_kernel,
        grid=(s // tm,),
        in_specs=[pl.BlockSpec((tm, d), lambda i: (i, 0)),
                  pl.BlockSpec((1, d), lambda i: (0, 0)),
                  pl.BlockSpec((N_EXPERTS, d), lambda i: (0, 0)),
                  pl.BlockSpec((N_EXPERTS, 1), lambda i: (0, 0)),
                  pl.BlockSpec((tm, tm), lambda i: (0, 0))],
        out_specs=[pl.BlockSpec((tm, d // 2), lambda i: (i, 0)), kt(), kt(), kt(),
                   pl.BlockSpec((N_EXPERTS, 128), lambda i: (0, 0))],
        out_shape=[jax.ShapeDtypeStruct((s, d // 2), U32),
                   jax.ShapeDtypeStruct((TOP_K, s), I32),
                   jax.ShapeDtypeStruct((TOP_K, s), F32),
                   jax.ShapeDtypeStruct((TOP_K, s), I32),
                   jax.ShapeDtypeStruct((N_EXPERTS, 128), F32)],
        scratch_shapes=[pltpu.VMEM((N_EXPERTS, 128), F32)],
        compiler_params=_cparams(("arbitrary",)),
        name="router",
    )(h, g.reshape(1, d), w_router.T, b_router.reshape(N_EXPERTS, 1), tri)


def _experts_kernel(ge_ref, gx_ref, nsub_ref, nrow_ref,
                    x_ref, wg_ref, wu_ref, wd_ref, bg_ref, bu_ref, bd_ref, o_ref, xb_ref):
    g = pl.program_id(0)
    f = pl.program_id(1)
    ns = nsub_ref[g]
    half = x_ref.shape[1]

    def compute(rows, first):
        if first:
            w = x_ref[0:rows, :]
            valid = lax.broadcasted_iota(I32, w.shape, 0) < nrow_ref[g]
            lo = jnp.where(valid, pltpu.bitcast(w << 16, F32), 0.0).astype(BF16)
            hi = jnp.where(valid, pltpu.bitcast(w & jnp.uint32(0xFFFF0000), F32), 0.0).astype(BF16)
            xj = jnp.concatenate([lo, hi], axis=1)
            xb_ref[0:rows, :] = xj
        else:
            xj = xb_ref[0:rows, :]
        gate = jnp.dot(xj, wg_ref[0].astype(BF16), preferred_element_type=F32) + bg_ref[0]
        up = jnp.dot(xj, wu_ref[0].astype(BF16), preferred_element_type=F32) + bu_ref[0]
        gate = jnp.minimum(gate, SWIGLU_LIMIT)
        up = jnp.clip(up, -SWIGLU_LIMIT, SWIGLU_LIMIT)
        act = (up + 1.0) * gate * jax.nn.sigmoid(SWIGLU_ALPHA * gate)
        y = jnp.dot(act.astype(BF16), wd_ref[0].astype(BF16), preferred_element_type=F32)
        if first:
            o_ref[0:rows, :] = y + bd_ref[0]
            if rows < GROUP_ROWS:
                o_ref[rows:, :] = jnp.zeros((GROUP_ROWS - rows, o_ref.shape[1]), F32)
        else:
            o_ref[0:rows, :] += y

    for n in range(1, GROUP_ROWS // SUB_ROWS + 1):
        @pl.when(jnp.logical_and(ns == n, f == 0))
        def _(n=n):
            compute(n * SUB_ROWS, True)

        @pl.when(jnp.logical_and(ns == n, f > 0))
        def _(n=n):
            compute(n * SUB_ROWS, False)

    @pl.when(jnp.logical_and(ns == 0, f == 0))
    def _():
        o_ref[...] = jnp.zeros(o_ref.shape, F32)


def _experts(xs, w_gu, b_gu, w_down, b_down, g_e, g_x, g_nsub, g_nrow):
    m_pad, half = xs.shape
    d = 2 * half
    e, _, ff2 = w_gu.shape
    ff = ff2 // 2
    nf = ff // FF_TILE
    ng = m_pad // GROUP_ROWS

    def ft(g, f, ns):
        return jnp.where(ns[g] > 0, f, nf - 1)

    grid_spec = pltpu.PrefetchScalarGridSpec(
        num_scalar_prefetch=4,
        grid=(ng, nf),
        in_specs=[pl.BlockSpec((GROUP_ROWS, half), lambda g, f, ge, gx, ns, nr: (gx[g], 0)),
                  pl.BlockSpec((1, d, FF_TILE), lambda g, f, ge, gx, ns, nr: (ge[g], 0, ft(g, f, ns))),
                  pl.BlockSpec((1, d, FF_TILE), lambda g, f, ge, gx, ns, nr: (ge[g], 0, nf + ft(g, f, ns))),
                  pl.BlockSpec((1, FF_TILE, d), lambda g, f, ge, gx, ns, nr: (ge[g], ft(g, f, ns), 0)),
                  pl.BlockSpec((1, 1, FF_TILE), lambda g, f, ge, gx, ns, nr: (ge[g], 0, ft(g, f, ns))),
                  pl.BlockSpec((1, 1, FF_TILE), lambda g, f, ge, gx, ns, nr: (ge[g], 0, nf + ft(g, f, ns))),
                  pl.BlockSpec((1, 1, d), lambda g, f, ge, gx, ns, nr: (ge[g], 0, 0))],
        out_specs=pl.BlockSpec((GROUP_ROWS, d), lambda g, f, ge, gx, ns, nr: (g, 0)),
        scratch_shapes=[pltpu.VMEM((GROUP_ROWS, d), BF16)],
    )
    return pl.pallas_call(
        _experts_kernel,
        grid_spec=grid_spec,
        out_shape=jax.ShapeDtypeStruct((m_pad, d), F32),
        compiler_params=_cparams(("arbitrary", "arbitrary")),
        name="experts",
    )(g_e, g_x, g_nsub, g_nrow, xs, w_gu, w_gu, w_down,
      b_gu.reshape(e, 1, ff2), b_gu.reshape(e, 1, ff2), b_down.reshape(e, 1, d))


ROW_DMA_UNROLL = 8


def _for_rows(n_rows, fn):
    def body(r, carry):
        for k in range(TOP_K):
            fn(r, k)
        return carry
    lax.fori_loop(0, n_rows, body, 0, unroll=ROW_DMA_UNROLL)


def _dispatch_kernel(dest_ref, x_ref, xs_ref, sem):
    tm = x_ref.shape[0]

    def row_copy(r, k):
        return pltpu.make_async_copy(x_ref.at[pl.ds(r, 1)], xs_ref.at[pl.ds(dest_ref[k, r], 1)], sem)

    _for_rows(tm, lambda r, k: row_copy(r, k).start())
    _for_rows(tm, lambda r, k: row_copy(r, k).wait())


def _dispatch(xp, dest, m_pad, tm):
    s, w = xp.shape
    tm = min(tm, s)
    return pl.pallas_call(
        _dispatch_kernel,
        grid=(s // tm,),
        in_specs=[pl.BlockSpec((TOP_K, tm), lambda i: (0, i), memory_space=pltpu.SMEM),
                  pl.BlockSpec((tm, w), lambda i: (i, 0))],
        out_specs=pl.BlockSpec(memory_space=pl.ANY),
        out_shape=jax.ShapeDtypeStruct((m_pad, w), U32),
        scratch_shapes=[pltpu.SemaphoreType.DMA],
        compiler_params=_cparams(("arbitrary",)),
        name="dispatch",
    )(dest, xp)


def _combine_kernel(dest_ref, dnext_ref, gate_ref, h_ref, g_ref, y_ref, o_ref, buf_ref, sem):
    i = pl.program_id(0)
    tm = h_ref.shape[0]
    cur = lax.rem(i, 2)

    def row_copy(d_ref, b, r, k):
        return pltpu.make_async_copy(y_ref.at[pl.ds(d_ref[k, r], 1)], buf_ref.at[b, k, pl.ds(r, 1)], sem.at[b])

    @pl.when(i == 0)
    def _():
        _for_rows(tm, lambda r, k: row_copy(dest_ref, 0, r, k).start())

    @pl.when(i + 1 < pl.num_programs(0))
    def _():
        _for_rows(tm, lambda r, k: row_copy(dnext_ref, 1 - cur, r, k).start())

    _for_rows(tm, lambda r, k: row_copy(dest_ref, cur, r, k).wait())
    acc = h_ref[...]
    gates = gate_ref[...]
    for k in range(TOP_K):
        acc = acc + buf_ref[cur, k] * gates[:, k:k + 1]
    o_ref[...] = _rms(acc, g_ref[...], EPS)


def _combine(y, dest, gates_t, h, g, tm):
    s, d = h.shape
    tm = min(tm, s)
    nb = s // tm
    return pl.pallas_call(
        _combine_kernel,
        grid=(nb,),
        in_specs=[pl.BlockSpec((TOP_K, tm), lambda i: (0, i), memory_space=pltpu.SMEM),
                  pl.BlockSpec((TOP_K, tm), lambda i: (0, jnp.minimum(i + 1, nb - 1)), memory_space=pltpu.SMEM),
                  pl.BlockSpec((tm, TOP_K), lambda i: (i, 0)),
                  pl.BlockSpec((tm, d), lambda i: (i, 0)),
                  pl.BlockSpec((1, d), lambda i: (0, 0)),
                  pl.BlockSpec(memory_space=pl.ANY)],
        out_specs=pl.BlockSpec((tm, d), lambda i: (i, 0)),
        out_shape=jax.ShapeDtypeStruct((s, d), F32),
        scratch_shapes=[pltpu.VMEM((2, TOP_K, tm, d), F32), pltpu.SemaphoreType.DMA((2,))],
        compiler_params=_cparams(("arbitrary",)),
        name="combine",
    )(dest, dest, gates_t, h, g.reshape(1, d), y)


def _routing_tables(idx, rank, counts, n_groups):
    seg_groups = (counts + GROUP_ROWS - 1) // GROUP_ROWS
    grp_end = jnp.cumsum(seg_groups)
    grp_start = grp_end - seg_groups
    row_start = grp_start * GROUP_ROWS
    hit = idx[..., None] == jnp.arange(N_EXPERTS, dtype=I32)
    dest = jnp.sum(jnp.where(hit, row_start, 0), axis=-1) + rank
    total = grp_end[-1]
    gid = jnp.arange(n_groups, dtype=I32)
    used = gid < total
    gclamp = jnp.minimum(gid, total - 1)
    g_e = jnp.minimum(jnp.searchsorted(grp_end, gclamp, side="right"), N_EXPERTS - 1).astype(I32)
    rows = jnp.clip(counts[g_e] - (gclamp - grp_start[g_e]) * GROUP_ROWS, 0, GROUP_ROWS)
    g_nrow = jnp.where(used, rows, 0).astype(I32)
    g_nsub = (g_nrow + SUB_ROWS - 1) // SUB_ROWS
    return dest.astype(I32), g_e, gclamp.astype(I32), g_nsub.astype(I32), g_nrow


def kernel(x, mem, norm_mix_g, w_in, conv_w, conv_b, w_rg_a, b_rg_a, w_rg_x, b_rg_x, rg_lambda, rec_norm_g,
           lambda_q1, lambda_k1, lambda_q2, lambda_k2, subln_g, w_out, norm_cross_g, norm_mem_g, w_cq, w_ckv,
           w_co, norm_ffn_g, w_router, b_router, w_gate_up, b_gate_up, w_down, b_down, norm_final_g):
    b, s, d = x.shape
    assert b == 1 and w_in.shape[0] == 1
    aw = DA_HEADS * 2 * DA_HEAD_DIM
    h0 = x.reshape(s, d)

    w_in_b = w_in[0].astype(BF16)
    ones = lambda n: jnp.ones((1, n), F32)
    q_scale = jnp.concatenate([jnp.full((1, aw), (DA_HEAD_DIM ** -0.5) * LOG2E, F32), ones(2 * aw)], axis=1)
    qkv = _norm_matmul(h0, norm_mix_g[0], w_in_b[:, :3 * aw], q_scale, BF16, 1024, 1024)
    rg = _norm_matmul(h0, norm_mix_g[0], w_in_b[:, 3 * aw:], ones(w_in_b.shape[1] - 3 * aw), F32, 1024, 1024)
    lam_params = jnp.stack([lambda_q1[0], lambda_k1[0], lambda_q2[0], lambda_k2[0]])
    attn = _diff_attn(qkv, lam_params, subln_g[0], 2048, 8)
    rec = _rglru(rg, conv_w[0], conv_b[0], w_rg_a[0], b_rg_a[0], w_rg_x[0], b_rg_x[0],
                 rg_lambda[0], rec_norm_g[0], 256)
    h1 = _matmul_res(attn, 0, rec, 0, w_out[0].astype(BF16), h0, 1024, 1024)

    nm = mem.shape[1]
    kv = _norm_matmul(mem.reshape(nm, d), norm_mem_g[0], w_ckv[0].astype(BF16), ones(2 * d), BF16, 256, 1024)
    o = _cross_attn(h1, norm_cross_g[0], w_cq[0].astype(BF16), kv, 512)
    h2 = _matmul_res(o, 0, o, 1, w_co[0].astype(BF16), h1, 1024, 1024)

    xp, idx, gates, rank, cnt = _router(h2, norm_ffn_g[0], w_router[0], b_router[0], 512)
    n_groups = (s * TOP_K) // GROUP_ROWS + N_EXPERTS
    dest, g_e, g_x, g_nsub, g_nrow = _routing_tables(idx, rank, cnt[:, 0].astype(I32), n_groups)
    xs = _dispatch(xp, dest, n_groups * GROUP_ROWS, 256)
    y = _experts(xs, w_gate_up[0], b_gate_up[0], w_down[0], b_down[0], g_e, g_x, g_nsub, g_nrow)
    out = _combine(y, dest, gates.T, h2, norm_final_g, 128)
    return out.reshape(b, s, d)
```

```python
import functools
import math

import jax
import jax.numpy as jnp
from jax import lax
from jax.experimental import pallas as pl
from jax.experimental.pallas import tpu as pltpu

F32 = jnp.float32
BF16 = jnp.bfloat16
U32 = jnp.uint32
I32 = jnp.int32

DA_HEAD_DIM = 128
DA_HEADS = 4
REC_BLOCKS = 8
CONV_W = 4
RG_C = 8.0
CROSS_HEADS = 4
N_EXPERTS = 32
TOP_K = 4
SWIGLU_LIMIT = 7.0
SWIGLU_ALPHA = 1.702
EPS = 1e-6
DA_EPS = 1e-5
LAMBDA_INIT = 0.8 - 0.6 * math.exp(-0.3 * 0)
LOG2E = 1.4426950408889634

V7X_VMEM_BYTES = 64 * 1024 * 1024
VMEM_LIMIT = V7X_VMEM_BYTES - 8 * 1024 * 1024

GROUP_ROWS = 1024
SUB_ROWS = 128
FF_TILE = 256


def _cparams(sem):
    return pltpu.CompilerParams(dimension_semantics=sem, vmem_limit_bytes=VMEM_LIMIT)


def _rms(x, g, eps):
    return x * lax.rsqrt(jnp.mean(x * x, axis=-1, keepdims=True) + eps) * g


def _norm_matmul_kernel(x_ref, g_ref, w_ref, cs_ref, o_ref, xn_ref):
    @pl.when(pl.program_id(1) == 0)
    def _():
        xn_ref[...] = _rms(x_ref[...], g_ref[...], EPS).astype(BF16)

    acc = jnp.dot(xn_ref[...], w_ref[...], preferred_element_type=F32)
    o_ref[...] = (acc * cs_ref[...]).astype(o_ref.dtype)


def _norm_matmul(x, g, w, col_scale, out_dtype, tm, tn):
    m, k = x.shape
    n = w.shape[1]
    tm, tn = min(tm, m), min(tn, n)
    return pl.pallas_call(
        _norm_matmul_kernel,
        grid=(m // tm, n // tn),
        in_specs=[pl.BlockSpec((tm, k), lambda i, j: (i, 0)),
                  pl.BlockSpec((1, k), lambda i, j: (0, 0)),
                  pl.BlockSpec((k, tn), lambda i, j: (0, j)),
                  pl.BlockSpec((1, tn), lambda i, j: (0, j))],
        out_specs=pl.BlockSpec((tm, tn), lambda i, j: (i, j)),
        out_shape=jax.ShapeDtypeStruct((m, n), out_dtype),
        scratch_shapes=[pltpu.VMEM((tm, k), BF16)],
        compiler_params=_cparams(("parallel", "arbitrary")),
        name="norm_matmul",
    )(x, g.reshape(1, k), w, col_scale)


def _in_proj_kernel(x_ref, g_ref, w_ref, cs_ref, qkv_ref, rg_ref, xn_ref, *, n_qkv_tiles):
    j = pl.program_id(1)

    @pl.when(j == 0)
    def _():
        xn_ref[...] = _rms(x_ref[...], g_ref[...], EPS).astype(BF16)

    acc = jnp.dot(xn_ref[...], w_ref[...], preferred_element_type=F32)

    @pl.when(j < n_qkv_tiles)
    def _():
        qkv_ref[...] = (acc * cs_ref[...]).astype(qkv_ref.dtype)

    @pl.when(j >= n_qkv_tiles)
    def _():
        rg_ref[...] = acc


def _in_proj(x, g, w, q_scale, n_qkv, tm, tn):
    m, k = x.shape
    n = w.shape[1]
    tm = min(tm, m)
    assert n_qkv % tn == 0 and (n - n_qkv) % tn == 0
    nq = n_qkv // tn
    return pl.pallas_call(
        functools.partial(_in_proj_kernel, n_qkv_tiles=nq),
        grid=(m // tm, n // tn),
        in_specs=[pl.BlockSpec((tm, k), lambda i, j: (i, 0)),
                  pl.BlockSpec((1, k), lambda i, j: (0, 0)),
                  pl.BlockSpec((k, tn), lambda i, j: (0, j)),
                  pl.BlockSpec((1, tn), lambda i, j: (0, jnp.minimum(j, nq - 1)))],
        out_specs=[pl.BlockSpec((tm, tn), lambda i, j: (i, jnp.minimum(j, nq - 1))),
                   pl.BlockSpec((tm, tn), lambda i, j: (i, jnp.maximum(j - nq, 0)))],
        out_shape=[jax.ShapeDtypeStruct((m, n_qkv), BF16), jax.ShapeDtypeStruct((m, n - n_qkv), F32)],
        scratch_shapes=[pltpu.VMEM((tm, k), BF16)],
        compiler_params=_cparams(("parallel", "arbitrary")),
        name="in_proj",
    )(x, g.reshape(1, k), w, q_scale)


def _matmul_res_kernel(a0_ref, a1_ref, w0_ref, w1_ref, r_ref, o_ref):
    acc = jnp.dot(a0_ref[...], w0_ref[...], preferred_element_type=F32)
    acc += jnp.dot(a1_ref[...], w1_ref[...], preferred_element_type=F32)
    o_ref[...] = r_ref[...] + acc


def _matmul_res(a0, c0, a1, c1, w, res, tm, tn):
    m, n = res.shape
    kh = w.shape[0] // 2
    tm, tn = min(tm, m), min(tn, n)
    return pl.pallas_call(
        _matmul_res_kernel,
        grid=(m // tm, n // tn),
        in_specs=[pl.BlockSpec((tm, kh), lambda i, j: (i, c0)),
                  pl.BlockSpec((tm, kh), lambda i, j: (i, c1)),
                  pl.BlockSpec((kh, tn), lambda i, j: (0, j)),
                  pl.BlockSpec((kh, tn), lambda i, j: (1, j)),
                  pl.BlockSpec((tm, tn), lambda i, j: (i, j))],
        out_specs=pl.BlockSpec((tm, tn), lambda i, j: (i, j)),
        out_shape=jax.ShapeDtypeStruct((m, n), F32),
        compiler_params=_cparams(("parallel", "parallel")),
        name="matmul_res",
    )(a0, a1, w, w, res)


def _diff_attn_kernel(qi_ref, ki_ref, q_ref, k_ref, v_ref, lam_ref, g_ref, o_ref,
                      m_ref, l_ref, acc_ref, *, t, n_split):
    p = pl.program_id(1)
    qi = qi_ref[p]
    ki = ki_ref[p]
    d = DA_HEAD_DIM
    rs = t // n_split

    @pl.when(ki == 0)
    def _():
        m_ref[...] = jnp.full(m_ref.shape, -jnp.inf, F32)
        l_ref[...] = jnp.zeros(l_ref.shape, F32)
        acc_ref[...] = jnp.zeros(acc_ref.shape, F32)

    def step(diag):
        for h in range(n_split):
            r0 = h * rs
            kw = r0 + rs if diag else t
            v = v_ref[0:kw, :]
            for i in range(2):
                q = q_ref[r0:r0 + rs, i * d:(i + 1) * d]
                k = k_ref[0:kw, i * d:(i + 1) * d]
                s = lax.dot_general(q, k, (((1,), (1,)), ((), ())), preferred_element_type=F32)
                if diag:
                    row = r0 + lax.broadcasted_iota(I32, (rs, kw), 0)
                    col = lax.broadcasted_iota(I32, (rs, kw), 1)
                    s = jnp.where(col <= row, s, -jnp.inf)
                m_old = m_ref[i, r0:r0 + rs, :1]
                m_new = jnp.maximum(m_old, jnp.max(s, axis=-1, keepdims=True))
                alpha = jnp.exp2(m_old - m_new)
                pr = jnp.exp2(s - m_new)
                l_ref[i, r0:r0 + rs, :] = jnp.broadcast_to(
                    alpha * l_ref[i, r0:r0 + rs, :1] + jnp.sum(pr, axis=-1, keepdims=True), (rs, 128))
                m_ref[i, r0:r0 + rs, :] = jnp.broadcast_to(m_new, (rs, 128))
                acc_ref[i, r0:r0 + rs, :] = alpha * acc_ref[i, r0:r0 + rs, :] + jnp.dot(
                    pr.astype(BF16), v, preferred_element_type=F32)

    @pl.when(ki == qi)
    def _():
        step(True)

    @pl.when(ki < qi)
    def _():
        step(False)

    @pl.when(ki == qi)
    def _():
        lp = lam_ref[...]
        lam = (jnp.exp(jnp.sum(lp[0:1] * lp[1:2], axis=-1, keepdims=True))
               - jnp.exp(jnp.sum(lp[2:3] * lp[3:4], axis=-1, keepdims=True)) + LAMBDA_INIT)
        o = acc_ref[0] / l_ref[0][:, :1] - lam * (acc_ref[1] / l_ref[1][:, :1])
        o = _rms(o, g_ref[...], DA_EPS) * (1.0 - LAMBDA_INIT)
        o_ref[...] = o.astype(o_ref.dtype)


def _diff_attn(qkv, lam_params, subln_g, t, n_split):
    s = qkv.shape[0]
    dv = 2 * DA_HEAD_DIM
    t = min(t, s)
    n_split = min(n_split, t // 128)
    assert s % t == 0 and t % n_split == 0
    nq = s // t
    pairs = [(a, b) for a in range(nq) for b in range(a + 1)]
    qi = jnp.asarray([a for a, _ in pairs], I32)
    ki = jnp.asarray([b for _, b in pairs], I32)
    grid_spec = pltpu.PrefetchScalarGridSpec(
        num_scalar_prefetch=2,
        grid=(DA_HEADS, len(pairs)),
        in_specs=[pl.BlockSpec((t, dv), lambda h, p, qi, ki: (qi[p], h)),
                  pl.BlockSpec((t, dv), lambda h, p, qi, ki: (ki[p], DA_HEADS + h)),
                  pl.BlockSpec((t, dv), lambda h, p, qi, ki: (ki[p], 2 * DA_HEADS + h)),
                  pl.BlockSpec((4, DA_HEAD_DIM), lambda h, p, qi, ki: (0, 0)),
                  pl.BlockSpec((1, dv), lambda h, p, qi, ki: (0, 0))],
        out_specs=pl.BlockSpec((t, dv), lambda h, p, qi, ki: (qi[p], h)),
        scratch_shapes=[pltpu.VMEM((2, t, 128), F32),
                        pltpu.VMEM((2, t, 128), F32),
                        pltpu.VMEM((2, t, dv), F32)],
    )
    return pl.pallas_call(
        functools.partial(_diff_attn_kernel, t=t, n_split=n_split),
        grid_spec=grid_spec,
        out_shape=jax.ShapeDtypeStruct((s, DA_HEADS * dv), BF16),
        compiler_params=_cparams(("parallel", "arbitrary")),
        name="diff_attn",
    )(qi, ki, qkv, qkv, qkv, lam_params, subln_g.reshape(1, dv))


def _rglru_kernel(xr_ref, xg_ref, cw_ref, cb_ref, wa_ref, ba_ref, wx_ref, bx_ref, lam_ref, g_ref,
                  o_ref, ext_ref, h_ref, *, t):
    @pl.when(pl.program_id(0) == 0)
    def _():
        ext_ref[0:8, :] = jnp.zeros((8, ext_ref.shape[1]), F32)
        h_ref[...] = jnp.zeros(h_ref.shape, F32)

    xr = xr_ref[...]
    c = xr.shape[1]
    bd = c // REC_BLOCKS
    ext_ref[8:8 + t, :] = xr
    cw = cw_ref[...]
    xc = xr * cw[CONV_W - 1:CONV_W] + cb_ref[...]
    for dlt in range(1, CONV_W):
        xc = xc + ext_ref[8 - dlt:8 - dlt + t, :] * cw[CONV_W - 1 - dlt:CONV_W - dlt]
    ext_ref[0:8, :] = xr[t - 8:t]

    xcb = xc.astype(BF16)
    ra, ri = [], []
    for n in range(REC_BLOCKS):
        xb = xcb[:, n * bd:(n + 1) * bd]
        ra.append(jnp.dot(xb, wa_ref[n], preferred_element_type=F32))
        ri.append(jnp.dot(xb, wx_ref[n], preferred_element_type=F32))
    r = jax.nn.sigmoid(jnp.concatenate(ra, axis=1) + ba_ref[...])
    ig = jax.nn.sigmoid(jnp.concatenate(ri, axis=1) + bx_ref[...])
    log_a = (-RG_C * jax.nn.softplus(-lam_ref[...])) * r
    a = jnp.exp(log_a)
    u = -jnp.tanh(log_a) * (a * a + 1.0)
    root = jnp.where(u > 0.0, u * lax.rsqrt(u), 0.0)
    b = root * (ig * xc)

    a = a.reshape(t // 8, 8, c)
    b = b.reshape(t // 8, 8, c)
    sub = lax.broadcasted_iota(I32, a.shape, 1)
    for dlt in (1, 2, 4):
        keep = sub < dlt
        a_sh = jnp.where(keep, 1.0, pltpu.roll(a, dlt, axis=1))
        b_sh = jnp.where(keep, 0.0, pltpu.roll(b, dlt, axis=1))
        b = a * b_sh + b
        a = a * a_sh
    h_in = h_ref[...]
    hs = []
    for j in range(t // 8):
        hj = a[j] * h_in + b[j]
        hs.append(hj)
        h_in = jnp.broadcast_to(hj[7:8], hj.shape)
    h = jnp.concatenate(hs, axis=0)
    h_ref[...] = h_in

    y = h * jax.nn.gelu(xg_ref[...])
    o_ref[...] = _rms(y, g_ref[...], EPS).astype(o_ref.dtype)


def _rglru(rg, conv_w, conv_b, w_a, b_a, w_x, b_x, rg_lambda, rec_norm_g, t):
    s = rg.shape[0]
    c = rg.shape[1] // 2
    t = min(t, s)
    bd = c // REC_BLOCKS
    vec = lambda: pl.BlockSpec((1, c), lambda i: (0, 0))
    return pl.pallas_call(
        functools.partial(_rglru_kernel, t=t),
        grid=(s // t,),
        in_specs=[pl.BlockSpec((t, c), lambda i: (i, 0)),
                  pl.BlockSpec((t, c), lambda i: (i, 1)),
                  pl.BlockSpec((CONV_W, c), lambda i: (0, 0)),
                  vec(),
                  pl.BlockSpec((REC_BLOCKS, bd, bd), lambda i: (0, 0, 0)),
                  vec(),
                  pl.BlockSpec((REC_BLOCKS, bd, bd), lambda i: (0, 0, 0)),
                  vec(), vec(), vec()],
        out_specs=pl.BlockSpec((t, c), lambda i: (i, 0)),
        out_shape=jax.ShapeDtypeStruct((s, c), BF16),
        scratch_shapes=[pltpu.VMEM((t + 8, c), F32), pltpu.VMEM((8, c), F32)],
        compiler_params=_cparams(("arbitrary",)),
        name="rglru",
    )(rg, rg, conv_w, conv_b.reshape(1, c), w_a.astype(BF16), b_a.reshape(1, c),
      w_x.astype(BF16), b_x.reshape(1, c), rg_lambda.reshape(1, c), rec_norm_g.reshape(1, c))


def _cross_attn_kernel(h_ref, g_ref, wq_ref, kv_ref, o_ref):
    d = h_ref.shape[1]
    hd = d // CROSS_HEADS
    c = (hd ** -0.5) * LOG2E
    n = _rms(h_ref[...], g_ref[...], EPS).astype(BF16)
    q = jnp.dot(n, wq_ref[...], preferred_element_type=F32).astype(BF16)
    for hh in range(CROSS_HEADS):
        qh = q[:, hh * hd:(hh + 1) * hd]
        kh = kv_ref[:, hh * hd:(hh + 1) * hd]
        vh = kv_ref[:, d + hh * hd:d + (hh + 1) * hd]
        s = lax.dot_general(qh, kh, (((1,), (1,)), ((), ())), preferred_element_type=F32) * c
        pr = jnp.exp2(s - jnp.max(s, axis=-1, keepdims=True))
        l = jnp.sum(pr, axis=-1, keepdims=True)
        o = jnp.dot(pr.astype(BF16), vh, preferred_element_type=F32) / l
        o_ref[:, hh * hd:(hh + 1) * hd] = o.astype(o_ref.dtype)


def _cross_attn(h, g, wq, kv, tm):
    s, d = h.shape
    nm = kv.shape[0]
    tm = min(tm, s)
    return pl.pallas_call(
        _cross_attn_kernel,
        grid=(s // tm,),
        in_specs=[pl.BlockSpec((tm, d), lambda i: (i, 0)),
                  pl.BlockSpec((1, d), lambda i: (0, 0)),
                  pl.BlockSpec((d, d), lambda i: (0, 0)),
                  pl.BlockSpec((nm, 2 * d), lambda i: (0, 0))],
        out_specs=pl.BlockSpec((tm, d), lambda i: (i, 0)),
        out_shape=jax.ShapeDtypeStruct((s, d), BF16),
        compiler_params=_cparams(("parallel",)),
        name="cross_attn",
    )(h, g.reshape(1, d), wq, kv)


def _router_kernel(h_ref, g_ref, wrt_ref, br_ref, tri_ref,
                   xp_ref, idx_ref, gate_ref, rank_ref, cnt_ref, carry_ref):
    @pl.when(pl.program_id(0) == 0)
    def _():
        carry_ref[...] = jnp.zeros(carry_ref.shape, F32)

    tm, d = h_ref.shape
    half = d // 2
    hn = _rms(h_ref[...], g_ref[...], EPS)
    bits = pltpu.bitcast(hn.astype(BF16).astype(F32), U32)
    xp_ref[...] = (bits[:, :half] >> 16) | (bits[:, half:] & jnp.uint32(0xFFFF0000))

    logits = lax.dot_general(wrt_ref[...], hn, (((1,), (1,)), ((), ())),
                             precision=lax.Precision.HIGHEST,
                             preferred_element_type=F32) + br_ref[...]
    e_iota = lax.broadcasted_iota(I32, logits.shape, 0)
    vals, idxs, hits = [], [], []
    l = logits
    for _ in range(TOP_K):
        mx = jnp.max(l, axis=0, keepdims=True)
        ik = jnp.min(jnp.where(l == mx, e_iota, N_EXPERTS), axis=0, keepdims=True)
        hit = e_iota == ik
        vals.append(mx)
        idxs.append(ik)
        hits.append(hit)
        l = jnp.where(hit, -jnp.inf, l)
    ex = [jnp.exp(v - vals[0]) for v in vals]
    den = ex[0] + ex[1] + ex[2] + ex[3]
    gate_ref[...] = jnp.concatenate([e / den for e in ex], axis=0)
    idx_ref[...] = jnp.concatenate(idxs, axis=0)

    onehot = jnp.zeros(logits.shape, F32)
    for hit in hits:
        onehot = onehot + jnp.where(hit, 1.0, 0.0)
    before = jnp.dot(onehot.astype(BF16), tri_ref[...], preferred_element_type=F32)
    before = before + carry_ref[:, :1]
    ranks = [jnp.sum(jnp.where(hit, before, 0.0), axis=0, keepdims=True) for hit in hits]
    rank_ref[...] = jnp.concatenate(ranks, axis=0).astype(I32)
    carry = carry_ref[...] + jnp.sum(onehot, axis=1, keepdims=True)
    carry_ref[...] = carry
    cnt_ref[...] = carry


def _router(h, g, w_router, b_router, tm):
    s, d = h.shape
    tm = min(tm, s)
    tri = jnp.triu(jnp.ones((tm, tm), BF16), 1)
    kt = lambda: pl.BlockSpec((TOP_K, tm), lambda i: (0, i))
    return pl.pallas_call(
        _router_kernel,
        grid=(s // tm,),
        in_specs=[pl.BlockSpec((tm, d), lambda i: (i, 0)),
                  pl.BlockSpec((1, d), lambda i: (0, 0)),
                  pl.BlockSpec((N_EXPERTS, d), lambda i: (0, 0)),
                  pl.BlockSpec((N_EXPERTS, 1), lambda i: (0, 0)),
                  pl.BlockSpec((tm, tm), lambda i: (0, 0))],
        out_specs=[pl.BlockSpec((tm, d // 2), lambda i: (i, 0)), kt(), kt(), kt(),
                   pl.BlockSpec((N_EXPERTS, 128), lambda i: (0, 0))],
        out_shape=[jax.ShapeDtypeStruct((s, d // 2), U32),
                   jax.ShapeDtypeStruct((TOP_K, s), I32),
                   jax.ShapeDtypeStruct((TOP_K, s), F32),
                   jax.ShapeDtypeStruct((TOP_K, s), I32),
                   jax.ShapeDtypeStruct((N_EXPERTS, 128), F32)],
        scratch_shapes=[pltpu.VMEM((N_EXPERTS, 128), F32)],
        compiler_params=_cparams(("arbitrary",)),
        name="router",
    )(h, g.reshape(1, d), w_router.T, b_router.reshape(N_EXPERTS, 1), tri)


def _experts_kernel(ge_ref, gx_ref, nsub_ref, nrow_ref,
                    x_ref, wg_ref, wu_ref, wd_ref, bg_ref, bu_ref, bd_ref, o_ref, xb_ref):
    g = pl.program_id(0)
    f = pl.program_id(1)
    ns = nsub_ref[g]
    half = x_ref.shape[1]

    def compute(rows, first):
        if first:
            w = x_ref[0:rows, :]
            valid = lax.broadcasted_iota(I32, w.shape, 0) < nrow_ref[g]
            lo = jnp.where(valid, pltpu.bitcast(w << 16, F32), 0.0).astype(BF16)
            hi = jnp.where(valid, pltpu.bitcast(w & jnp.uint32(0xFFFF0000), F32), 0.0).astype(BF16)
            xj = jnp.concatenate([lo, hi], axis=1)
            xb_ref[0:rows, :] = xj
        else:
            xj = xb_ref[0:rows, :]
        gate = jnp.dot(xj, wg_ref[0].astype(BF16), preferred_element_type=F32) + bg_ref[0]
        up = jnp.dot(xj, wu_ref[0].astype(BF16), preferred_element_type=F32) + bu_ref[0]
        gate = jnp.minimum(gate, SWIGLU_LIMIT)
        up = jnp.clip(up, -SWIGLU_LIMIT, SWIGLU_LIMIT)
        act = (up + 1.0) * gate * jax.nn.sigmoid(SWIGLU_ALPHA * gate)
        y = jnp.dot(act.astype(BF16), wd_ref[0].astype(BF16), preferred_element_type=F32)
        if first:
            o_ref[0:rows, :] = y + bd_ref[0]
            if rows < GROUP_ROWS:
                o_ref[rows:, :] = jnp.zeros((GROUP_ROWS - rows, o_ref.shape[1]), F32)
        else:
            o_ref[0:rows, :] += y

    for n in range(1, GROUP_ROWS // SUB_ROWS + 1):
        @pl.when(jnp.logical_and(ns == n, f == 0))
        def _(n=n):
            compute(n * SUB_ROWS, True)

        @pl.when(jnp.logical_and(ns == n, f > 0))
        def _(n=n):
            compute(n * SUB_ROWS, False)

    @pl.when(jnp.logical_and(ns == 0, f == 0))
    def _():
        o_ref[...] = jnp.zeros(o_ref.shape, F32)


def _experts(xs, w_gu, b_gu, w_down, b_down, g_e, g_x, g_nsub, g_nrow):
    m_pad, half = xs.shape
    d = 2 * half
    e, _, ff2 = w_gu.shape
    ff = ff2 // 2
    nf = ff // FF_TILE
    ng = m_pad // GROUP_ROWS

    def ft(g, f, ns):
        return jnp.where(ns[g] > 0, f, nf - 1)

    grid_spec = pltpu.PrefetchScalarGridSpec(
        num_scalar_prefetch=4,
        grid=(ng, nf),
        in_specs=[pl.BlockSpec((GROUP_ROWS, half), lambda g, f, ge, gx, ns, nr: (gx[g], 0)),
                  pl.BlockSpec((1, d, FF_TILE), lambda g, f, ge, gx, ns, nr: (ge[g], 0, ft(g, f, ns))),
                  pl.BlockSpec((1, d, FF_TILE), lambda g, f, ge, gx, ns, nr: (ge[g], 0, nf + ft(g, f, ns))),
                  pl.BlockSpec((1, FF_TILE, d), lambda g, f, ge, gx, ns, nr: (ge[g], ft(g, f, ns), 0)),
                  pl.BlockSpec((1, 1, FF_TILE), lambda g, f, ge, gx, ns, nr: (ge[g], 0, ft(g, f, ns))),
                  pl.BlockSpec((1, 1, FF_TILE), lambda g, f, ge, gx, ns, nr: (ge[g], 0, nf + ft(g, f, ns))),
                  pl.BlockSpec((1, 1, d), lambda g, f, ge, gx, ns, nr: (ge[g], 0, 0))],
        out_specs=pl.BlockSpec((GROUP_ROWS, d), lambda g, f, ge, gx, ns, nr: (g, 0)),
        scratch_shapes=[pltpu.VMEM((GROUP_ROWS, d), BF16)],
    )
    return pl.pallas_call(
        _experts_kernel,
        grid_spec=grid_spec,
        out_shape=jax.ShapeDtypeStruct((m_pad, d), F32),
        compiler_params=_cparams(("arbitrary", "arbitrary")),
        name="experts",
    )(g_e, g_x, g_nsub, g_nrow, xs, w_gu, w_gu, w_down,
      b_gu.reshape(e, 1, ff2), b_gu.reshape(e, 1, ff2), b_down.reshape(e, 1, d))


ROW_DMA_UNROLL = 8


def _for_rows(n_rows, fn):
    def body(r, carry):
        for k in range(TOP_K):
            fn(r, k)
        return carry
    lax.fori_loop(0, n_rows, body, 0, unroll=ROW_DMA_UNROLL)


def _dispatch_kernel(dest_ref, x_ref, xs_ref, sem):
    tm = x_ref.shape[0]

    def row_copy(r, k):
        return pltpu.make_async_copy(x_ref.at[pl.ds(r, 1)], xs_ref.at[pl.ds(dest_ref[k, r], 1)], sem)

    _for_rows(tm, lambda r, k: row_copy(r, k).start())
    _for_rows(tm, lambda r, k: row_copy(r, k).wait())


def _dispatch(xp, dest, m_pad, tm):
    s, w = xp.shape
    tm = min(tm, s)
    return pl.pallas_call(
        _dispatch_kernel,
        grid=(s // tm,),
        in_specs=[pl.BlockSpec((TOP_K, tm), lambda i: (0, i), memory_space=pltpu.SMEM),
                  pl.BlockSpec((tm, w), lambda i: (i, 0))],
        out_specs=pl.BlockSpec(memory_space=pl.ANY),
        out_shape=jax.ShapeDtypeStruct((m_pad, w), U32),
        scratch_shapes=[pltpu.SemaphoreType.DMA],
        compiler_params=_cparams(("arbitrary",)),
        name="dispatch",
    )(dest, xp)


def _combine_kernel(dest_ref, dnext_ref, gate_ref, h_ref, g_ref, y_ref, o_ref, buf_ref, sem):
    i = pl.program_id(0)
    tm = h_ref.shape[0]
    cur = lax.rem(i, 2)

    def row_copy(d_ref, b, r, k):
        return pltpu.make_async_copy(y_ref.at[pl.ds(d_ref[k, r], 1)], buf_ref.at[b, k, pl.ds(r, 1)], sem.at[b])

    @pl.when(i == 0)
    def _():
        _for_rows(tm, lambda r, k: row_copy(dest_ref, 0, r, k).start())

    @pl.when(i + 1 < pl.num_programs(0))
    def _():
        _for_rows(tm, lambda r, k: row_copy(dnext_ref, 1 - cur, r, k).start())

    _for_rows(tm, lambda r, k: row_copy(dest_ref, cur, r, k).wait())
    acc = h_ref[...]
    gates = gate_ref[...]
    for k in range(TOP_K):
        acc = acc + buf_ref[cur, k] * gates[:, k:k + 1]
    o_ref[...] = _rms(acc, g_ref[...], EPS)


def _combine(y, dest, gates_t, h, g, tm):
    s, d = h.shape
    tm = min(tm, s)
    nb = s // tm
    return pl.pallas_call(
        _combine_kernel,
        grid=(nb,),
        in_specs=[pl.BlockSpec((TOP_K, tm), lambda i: (0, i), memory_space=pltpu.SMEM),
                  pl.BlockSpec((TOP_K, tm), lambda i: (0, jnp.minimum(i + 1, nb - 1)), memory_space=pltpu.SMEM),
                  pl.BlockSpec((tm, TOP_K), lambda i: (i, 0)),
                  pl.BlockSpec((tm, d), lambda i: (i, 0)),
                  pl.BlockSpec((1, d), lambda i: (0, 0)),
                  pl.BlockSpec(memory_space=pl.ANY)],
        out_specs=pl.BlockSpec((tm, d), lambda i: (i, 0)),
        out_shape=jax.ShapeDtypeStruct((s, d), F32),
        scratch_shapes=[pltpu.VMEM((2, TOP_K, tm, d), F32), pltpu.SemaphoreType.DMA((2,))],
        compiler_params=_cparams(("arbitrary",)),
        name="combine",
    )(dest, dest, gates_t, h, g.reshape(1, d), y)


def _routing_tables(idx, rank, counts, n_groups):
    seg_groups = (counts + GROUP_ROWS - 1) // GROUP_ROWS
    grp_end = jnp.cumsum(seg_groups)
    grp_start = grp_end - seg_groups
    row_start = grp_start * GROUP_ROWS
    hit = idx[..., None] == jnp.arange(N_EXPERTS, dtype=I32)
    dest = jnp.sum(jnp.where(hit, row_start, 0), axis=-1) + rank
    total = grp_end[-1]
    gid = jnp.arange(n_groups, dtype=I32)
    used = gid < total
    gclamp = jnp.minimum(gid, total - 1)
    g_e = jnp.minimum(jnp.searchsorted(grp_end, gclamp, side="right"), N_EXPERTS - 1).astype(I32)
    rows = jnp.clip(counts[g_e] - (gclamp - grp_start[g_e]) * GROUP_ROWS, 0, GROUP_ROWS)
    g_nrow = jnp.where(used, rows, 0).astype(I32)
    g_nsub = (g_nrow + SUB_ROWS - 1) // SUB_ROWS
    return dest.astype(I32), g_e, gclamp.astype(I32), g_nsub.astype(I32), g_nrow


def kernel(x, mem, norm_mix_g, w_in, conv_w, conv_b, w_rg_a, b_rg_a, w_rg_x, b_rg_x, rg_lambda, rec_norm_g,
           lambda_q1, lambda_k1, lambda_q2, lambda_k2, subln_g, w_out, norm_cross_g, norm_mem_g, w_cq, w_ckv,
           w_co, norm_ffn_g, w_router, b_router, w_gate_up, b_gate_up, w_down, b_down, norm_final_g):
    b, s, d = x.shape
    assert b == 1 and w_in.shape[0] == 1
    aw = DA_HEADS * 2 * DA_HEAD_DIM
    h0 = x.reshape(s, d)

    w_in_b = w_in[0].astype(BF16)
    ones = lambda n: jnp.ones((1, n), F32)
    q_scale = jnp.concatenate([jnp.full((1, aw), (DA_HEAD_DIM ** -0.5) * LOG2E, F32), ones(2 * aw)], axis=1)
    qkv, rg = _in_proj(h0, norm_mix_g[0], w_in_b, q_scale, 3 * aw, 1024, 1024)
    lam_params = jnp.stack([lambda_q1[0], lambda_k1[0], lambda_q2[0], lambda_k2[0]])
    attn = _diff_attn(qkv, lam_params, subln_g[0], 2048, 8)
    rec = _rglru(rg, conv_w[0], conv_b[0], w_rg_a[0], b_rg_a[0], w_rg_x[0], b_rg_x[0],
                 rg_lambda[0], rec_norm_g[0], 256)
    h1 = _matmul_res(attn, 0, rec, 0, w_out[0].astype(BF16), h0, 1024, 1024)

    nm = mem.shape[1]
    kv = _norm_matmul(mem.reshape(nm, d), norm_mem_g[0], w_ckv[0].astype(BF16), ones(2 * d), BF16, 256, 1024)
    o = _cross_attn(h1, norm_cross_g[0], w_cq[0].astype(BF16), kv, 512)
    h2 = _matmul_res(o, 0, o, 1, w_co[0].astype(BF16), h1, 1024, 1024)

    xp, idx, gates, rank, cnt = _router(h2, norm_ffn_g[0], w_router[0], b_router[0], 512)
    n_groups = (s * TOP_K) // GROUP_ROWS + N_EXPERTS
    dest, g_e, g_x, g_nsub, g_nrow = _routing_tables(idx, rank, cnt[:, 0].astype(I32), n_groups)
    xs = _dispatch(xp, dest, n_groups * GROUP_ROWS, 256)
    y = _experts(xs, w_gate_up[0], b_gate_up[0], w_down[0], b_down[0], g_e, g_x, g_nsub, g_nrow)
    out = _combine(y, dest, gates.T, h2, norm_final_g, 128)
    return out.reshape(b, s, d)
```

```python
import functools
import math

import jax
import jax.numpy as jnp
from jax import lax
from jax.experimental import pallas as pl
from jax.experimental.pallas import tpu as pltpu

F32 = jnp.float32
BF16 = jnp.bfloat16
U32 = jnp.uint32
I32 = jnp.int32

DA_HEAD_DIM = 128
DA_HEADS = 4
REC_BLOCKS = 8
CONV_W = 4
RG_C = 8.0
CROSS_HEADS = 4
N_EXPERTS = 32
TOP_K = 4
SWIGLU_LIMIT = 7.0
SWIGLU_ALPHA = 1.702
EPS = 1e-6
DA_EPS = 1e-5
LAMBDA_INIT = 0.8 - 0.6 * math.exp(-0.3 * 0)
LOG2E = 1.4426950408889634

V7X_VMEM_BYTES = 64 * 1024 * 1024
VMEM_LIMIT = V7X_VMEM_BYTES - 8 * 1024 * 1024
EXPERTS_VMEM_LIMIT = V7X_VMEM_BYTES - 4 * 1024 * 1024

GROUP_ROWS = 1024
SUB_ROWS = 256
FF_TILE = 512


def _cparams(sem, vmem_limit=VMEM_LIMIT):
    return pltpu.CompilerParams(dimension_semantics=sem, vmem_limit_bytes=vmem_limit)


def _rms(x, g, eps):
    return x * lax.rsqrt(jnp.mean(x * x, axis=-1, keepdims=True) + eps) * g


def _norm_matmul_kernel(x_ref, g_ref, w_ref, cs_ref, o_ref, xn_ref):
    @pl.when(pl.program_id(1) == 0)
    def _():
        xn_ref[...] = _rms(x_ref[...], g_ref[...], EPS).astype(BF16)

    acc = jnp.dot(xn_ref[...], w_ref[...], preferred_element_type=F32)
    o_ref[...] = (acc * cs_ref[...]).astype(o_ref.dtype)


def _norm_matmul(x, g, w, col_scale, out_dtype, tm, tn):
    m, k = x.shape
    n = w.shape[1]
    tm, tn = min(tm, m), min(tn, n)
    return pl.pallas_call(
        _norm_matmul_kernel,
        grid=(m // tm, n // tn),
        in_specs=[pl.BlockSpec((tm, k), lambda i, j: (i, 0)),
                  pl.BlockSpec((1, k), lambda i, j: (0, 0)),
                  pl.BlockSpec((k, tn), lambda i, j: (0, j)),
                  pl.BlockSpec((1, tn), lambda i, j: (0, j))],
        out_specs=pl.BlockSpec((tm, tn), lambda i, j: (i, j)),
        out_shape=jax.ShapeDtypeStruct((m, n), out_dtype),
        scratch_shapes=[pltpu.VMEM((tm, k), BF16)],
        compiler_params=_cparams(("parallel", "arbitrary")),
        name="norm_matmul",
    )(x, g.reshape(1, k), w, col_scale)


def _in_proj_kernel(x_ref, g_ref, w_ref, cs_ref, qkv_ref, rg_ref, xn_ref, *, n_qkv_tiles):
    j = pl.program_id(1)

    @pl.when(j == 0)
    def _():
        xn_ref[...] = _rms(x_ref[...], g_ref[...], EPS).astype(BF16)

    acc = jnp.dot(xn_ref[...], w_ref[...], preferred_element_type=F32)

    @pl.when(j < n_qkv_tiles)
    def _():
        qkv_ref[...] = (acc * cs_ref[...]).astype(qkv_ref.dtype)

    @pl.when(j >= n_qkv_tiles)
    def _():
        rg_ref[...] = acc


def _in_proj(x, g, w, q_scale, n_qkv, tm, tn):
    m, k = x.shape
    n = w.shape[1]
    tm = min(tm, m)
    assert n_qkv % tn == 0 and (n - n_qkv) % tn == 0
    nq = n_qkv // tn
    return pl.pallas_call(
        functools.partial(_in_proj_kernel, n_qkv_tiles=nq),
        grid=(m // tm, n // tn),
        in_specs=[pl.BlockSpec((tm, k), lambda i, j: (i, 0)),
                  pl.BlockSpec((1, k), lambda i, j: (0, 0)),
                  pl.BlockSpec((k, tn), lambda i, j: (0, j)),
                  pl.BlockSpec((1, tn), lambda i, j: (0, jnp.minimum(j, nq - 1)))],
        out_specs=[pl.BlockSpec((tm, tn), lambda i, j: (i, jnp.minimum(j, nq - 1))),
                   pl.BlockSpec((tm, tn), lambda i, j: (i, jnp.maximum(j - nq, 0)))],
        out_shape=[jax.ShapeDtypeStruct((m, n_qkv), BF16), jax.ShapeDtypeStruct((m, n - n_qkv), F32)],
        scratch_shapes=[pltpu.VMEM((tm, k), BF16)],
        compiler_params=_cparams(("parallel", "arbitrary")),
        name="in_proj",
    )(x, g.reshape(1, k), w, q_scale)


def _matmul_res_kernel(a0_ref, a1_ref, w0_ref, w1_ref, r_ref, o_ref):
    acc = jnp.dot(a0_ref[...], w0_ref[...], preferred_element_type=F32)
    acc += jnp.dot(a1_ref[...], w1_ref[...], preferred_element_type=F32)
    o_ref[...] = r_ref[...] + acc


def _matmul_res(a0, c0, a1, c1, w, res, tm, tn):
    m, n = res.shape
    kh = w.shape[0] // 2
    tm, tn = min(tm, m), min(tn, n)
    return pl.pallas_call(
        _matmul_res_kernel,
        grid=(m // tm, n // tn),
        in_specs=[pl.BlockSpec((tm, kh), lambda i, j: (i, c0)),
                  pl.BlockSpec((tm, kh), lambda i, j: (i, c1)),
                  pl.BlockSpec((kh, tn), lambda i, j: (0, j)),
                  pl.BlockSpec((kh, tn), lambda i, j: (1, j)),
                  pl.BlockSpec((tm, tn), lambda i, j: (i, j))],
        out_specs=pl.BlockSpec((tm, tn), lambda i, j: (i, j)),
        out_shape=jax.ShapeDtypeStruct((m, n), F32),
        compiler_params=_cparams(("parallel", "parallel")),
        name="matmul_res",
    )(a0, a1, w, w, res)


def _diff_attn_kernel(qi_ref, ki_ref, q_ref, k_ref, v_ref, lam_ref, g_ref, o_ref,
                      m_ref, l_ref, acc_ref, *, t, n_split):
    p = pl.program_id(1)
    qi = qi_ref[p]
    ki = ki_ref[p]
    d = DA_HEAD_DIM
    rs = t // n_split

    @pl.when(ki == 0)
    def _():
        m_ref[...] = jnp.full(m_ref.shape, -jnp.inf, F32)
        l_ref[...] = jnp.zeros(l_ref.shape, F32)
        acc_ref[...] = jnp.zeros(acc_ref.shape, F32)

    def step(diag):
        for h in range(n_split):
            r0 = h * rs
            kw = r0 + rs if diag else t
            v = v_ref[0:kw, :]
            for i in range(2):
                q = q_ref[r0:r0 + rs, i * d:(i + 1) * d]
                k = k_ref[0:kw, i * d:(i + 1) * d]
                s = lax.dot_general(q, k, (((1,), (1,)), ((), ())), preferred_element_type=F32)
                if diag:
                    row = r0 + lax.broadcasted_iota(I32, (rs, kw), 0)
                    col = lax.broadcasted_iota(I32, (rs, kw), 1)
                    s = jnp.where(col <= row, s, -jnp.inf)
                m_old = m_ref[i, r0:r0 + rs, :1]
                m_new = jnp.maximum(m_old, jnp.max(s, axis=-1, keepdims=True))
                alpha = jnp.exp2(m_old - m_new)
                pr = jnp.exp2(s - m_new)
                l_ref[i, r0:r0 + rs, :] = jnp.broadcast_to(
                    alpha * l_ref[i, r0:r0 + rs, :1] + jnp.sum(pr, axis=-1, keepdims=True), (rs, 128))
                m_ref[i, r0:r0 + rs, :] = jnp.broadcast_to(m_new, (rs, 128))
                acc_ref[i, r0:r0 + rs, :] = alpha * acc_ref[i, r0:r0 + rs, :] + jnp.dot(
                    pr.astype(BF16), v, preferred_element_type=F32)

    @pl.when(ki == qi)
    def _():
        step(True)

    @pl.when(ki < qi)
    def _():
        step(False)

    @pl.when(ki == qi)
    def _():
        lp = lam_ref[...]
        lam = (jnp.exp(jnp.sum(lp[0:1] * lp[1:2], axis=-1, keepdims=True))
               - jnp.exp(jnp.sum(lp[2:3] * lp[3:4], axis=-1, keepdims=True)) + LAMBDA_INIT)
        o = acc_ref[0] / l_ref[0][:, :1] - lam * (acc_ref[1] / l_ref[1][:, :1])
        o = _rms(o, g_ref[...], DA_EPS) * (1.0 - LAMBDA_INIT)
        o_ref[...] = o.astype(o_ref.dtype)


def _diff_attn(qkv, lam_params, subln_g, t, n_split):
    s = qkv.shape[0]
    dv = 2 * DA_HEAD_DIM
    t = min(t, s)
    n_split = min(n_split, t // 128)
    assert s % t == 0 and t % n_split == 0
    nq = s // t
    pairs = [(a, b) for a in range(nq) for b in range(a + 1)]
    qi = jnp.asarray([a for a, _ in pairs], I32)
    ki = jnp.asarray([b for _, b in pairs], I32)
    grid_spec = pltpu.PrefetchScalarGridSpec(
        num_scalar_prefetch=2,
        grid=(DA_HEADS, len(pairs)),
        in_specs=[pl.BlockSpec((t, dv), lambda h, p, qi, ki: (qi[p], h)),
                  pl.BlockSpec((t, dv), lambda h, p, qi, ki: (ki[p], DA_HEADS + h)),
                  pl.BlockSpec((t, dv), lambda h, p, qi, ki: (ki[p], 2 * DA_HEADS + h)),
                  pl.BlockSpec((4, DA_HEAD_DIM), lambda h, p, qi, ki: (0, 0)),
                  pl.BlockSpec((1, dv), lambda h, p, qi, ki: (0, 0))],
        out_specs=pl.BlockSpec((t, dv), lambda h, p, qi, ki: (qi[p], h)),
        scratch_shapes=[pltpu.VMEM((2, t, 128), F32),
                        pltpu.VMEM((2, t, 128), F32),
                        pltpu.VMEM((2, t, dv), F32)],
    )
    return pl.pallas_call(
        functools.partial(_diff_attn_kernel, t=t, n_split=n_split),
        grid_spec=grid_spec,
        out_shape=jax.ShapeDtypeStruct((s, DA_HEADS * dv), BF16),
        compiler_params=_cparams(("parallel", "arbitrary")),
        name="diff_attn",
    )(qi, ki, qkv, qkv, qkv, lam_params, subln_g.reshape(1, dv))


def _rglru_kernel(xr_ref, xg_ref, cw_ref, cb_ref, wa_ref, ba_ref, wx_ref, bx_ref, lam_ref, g_ref,
                  o_ref, ext_ref, h_ref, *, t):
    @pl.when(pl.program_id(0) == 0)
    def _():
        ext_ref[0:8, :] = jnp.zeros((8, ext_ref.shape[1]), F32)
        h_ref[...] = jnp.zeros(h_ref.shape, F32)

    xr = xr_ref[...]
    c = xr.shape[1]
    bd = c // REC_BLOCKS
    ext_ref[8:8 + t, :] = xr
    cw = cw_ref[...]
    xc = xr * cw[CONV_W - 1:CONV_W] + cb_ref[...]
    for dlt in range(1, CONV_W):
        xc = xc + ext_ref[8 - dlt:8 - dlt + t, :] * cw[CONV_W - 1 - dlt:CONV_W - dlt]
    ext_ref[0:8, :] = xr[t - 8:t]

    xcb = xc.astype(BF16)
    ra, ri = [], []
    for n in range(REC_BLOCKS):
        xb = xcb[:, n * bd:(n + 1) * bd]
        ra.append(jnp.dot(xb, wa_ref[n], preferred_element_type=F32))
        ri.append(jnp.dot(xb, wx_ref[n], preferred_element_type=F32))
    r = jax.nn.sigmoid(jnp.concatenate(ra, axis=1) + ba_ref[...])
    ig = jax.nn.sigmoid(jnp.concatenate(ri, axis=1) + bx_ref[...])
    log_a = (-RG_C * jax.nn.softplus(-lam_ref[...])) * r
    a = jnp.exp(log_a)
    u = -jnp.tanh(log_a) * (a * a + 1.0)
    root = jnp.where(u > 0.0, u * lax.rsqrt(u), 0.0)
    b = root * (ig * xc)

    a = a.reshape(t // 8, 8, c)
    b = b.reshape(t // 8, 8, c)
    sub = lax.broadcasted_iota(I32, a.shape, 1)
    for dlt in (1, 2, 4):
        keep = sub < dlt
        a_sh = jnp.where(keep, 1.0, pltpu.roll(a, dlt, axis=1))
        b_sh = jnp.where(keep, 0.0, pltpu.roll(b, dlt, axis=1))
        b = a * b_sh + b
        a = a * a_sh
    h_in = h_ref[...]
    hs = []
    for j in range(t // 8):
        hj = a[j] * h_in + b[j]
        hs.append(hj)
        h_in = jnp.broadcast_to(hj[7:8], hj.shape)
    h = jnp.concatenate(hs, axis=0)
    h_ref[...] = h_in

    y = h * jax.nn.gelu(xg_ref[...])
    o_ref[...] = _rms(y, g_ref[...], EPS).astype(o_ref.dtype)


def _rglru(rg, conv_w, conv_b, w_a, b_a, w_x, b_x, rg_lambda, rec_norm_g, t):
    s = rg.shape[0]
    c = rg.shape[1] // 2
    t = min(t, s)
    bd = c // REC_BLOCKS
    vec = lambda: pl.BlockSpec((1, c), lambda i: (0, 0))
    return pl.pallas_call(
        functools.partial(_rglru_kernel, t=t),
        grid=(s // t,),
        in_specs=[pl.BlockSpec((t, c), lambda i: (i, 0)),
                  pl.BlockSpec((t, c), lambda i: (i, 1)),
                  pl.BlockSpec((CONV_W, c), lambda i: (0, 0)),
                  vec(),
                  pl.BlockSpec((REC_BLOCKS, bd, bd), lambda i: (0, 0, 0)),
                  vec(),
                  pl.BlockSpec((REC_BLOCKS, bd, bd), lambda i: (0, 0, 0)),
                  vec(), vec(), vec()],
        out_specs=pl.BlockSpec((t, c), lambda i: (i, 0)),
        out_shape=jax.ShapeDtypeStruct((s, c), BF16),
        scratch_shapes=[pltpu.VMEM((t + 8, c), F32), pltpu.VMEM((8, c), F32)],
        compiler_params=_cparams(("arbitrary",)),
        name="rglru",
    )(rg, rg, conv_w, conv_b.reshape(1, c), w_a.astype(BF16), b_a.reshape(1, c),
      w_x.astype(BF16), b_x.reshape(1, c), rg_lambda.reshape(1, c), rec_norm_g.reshape(1, c))


def _cross_attn_kernel(h_ref, g_ref, wq_ref, kv_ref, o_ref):
    d = h_ref.shape[1]
    hd = d // CROSS_HEADS
    c = (hd ** -0.5) * LOG2E
    n = _rms(h_ref[...], g_ref[...], EPS).astype(BF16)
    q = jnp.dot(n, wq_ref[...], preferred_element_type=F32).astype(BF16)
    for hh in range(CROSS_HEADS):
        qh = q[:, hh * hd:(hh + 1) * hd]
        kh = kv_ref[:, hh * hd:(hh + 1) * hd]
        vh = kv_ref[:, d + hh * hd:d + (hh + 1) * hd]
        s = lax.dot_general(qh, kh, (((1,), (1,)), ((), ())), preferred_element_type=F32) * c
        pr = jnp.exp2(s - jnp.max(s, axis=-1, keepdims=True))
        l = jnp.sum(pr, axis=-1, keepdims=True)
        o = jnp.dot(pr.astype(BF16), vh, preferred_element_type=F32) / l
        o_ref[:, hh * hd:(hh + 1) * hd] = o.astype(o_ref.dtype)


def _cross_attn(h, g, wq, kv, tm):
    s, d = h.shape
    nm = kv.shape[0]
    tm = min(tm, s)
    return pl.pallas_call(
        _cross_attn_kernel,
        grid=(s // tm,),
        in_specs=[pl.BlockSpec((tm, d), lambda i: (i, 0)),
                  pl.BlockSpec((1, d), lambda i: (0, 0)),
                  pl.BlockSpec((d, d), lambda i: (0, 0)),
                  pl.BlockSpec((nm, 2 * d), lambda i: (0, 0))],
        out_specs=pl.BlockSpec((tm, d), lambda i: (i, 0)),
        out_shape=jax.ShapeDtypeStruct((s, d), BF16),
        compiler_params=_cparams(("parallel",)),
        name="cross_attn",
    )(h, g.reshape(1, d), wq, kv)


def _router_kernel(h_ref, g_ref, wrt_ref, br_ref, tri_ref,
                   xp_ref, idx_ref, gate_ref, rank_ref, cnt_ref, carry_ref):
    @pl.when(pl.program_id(0) == 0)
    def _():
        carry_ref[...] = jnp.zeros(carry_ref.shape, F32)

    tm, d = h_ref.shape
    half = d // 2
    hn = _rms(h_ref[...], g_ref[...], EPS)
    bits = pltpu.bitcast(hn.astype(BF16).astype(F32), U32)
    xp_ref[...] = (bits[:, :half] >> 16) | (bits[:, half:] & jnp.uint32(0xFFFF0000))

    logits = lax.dot_general(wrt_ref[...], hn, (((1,), (1,)), ((), ())),
                             precision=lax.Precision.HIGHEST,
                             preferred_element_type=F32) + br_ref[...]
    e_iota = lax.broadcasted_iota(I32, logits.shape, 0)
    vals, idxs, hits = [], [], []
    l = logits
    for _ in range(TOP_K):
        mx = jnp.max(l, axis=0, keepdims=True)
        ik = jnp.min(jnp.where(l == mx, e_iota, N_EXPERTS), axis=0, keepdims=True)
        hit = e_iota == ik
        vals.append(mx)
        idxs.append(ik)
        hits.append(hit)
        l = jnp.where(hit, -jnp.inf, l)
    ex = [jnp.exp(v - vals[0]) for v in vals]
    den = ex[0] + ex[1] + ex[2] + ex[3]
    gate_ref[...] = jnp.concatenate([e / den for e in ex], axis=0)
    idx_ref[...] = jnp.concatenate(idxs, axis=0)

    onehot = jnp.zeros(logits.shape, F32)
    for hit in hits:
        onehot = onehot + jnp.where(hit, 1.0, 0.0)
    before = jnp.dot(onehot.astype(BF16), tri_ref[...], preferred_element_type=F32)
    before = before + carry_ref[:, :1]
    ranks = [jnp.sum(jnp.where(hit, before, 0.0), axis=0, keepdims=True) for hit in hits]
    rank_ref[...] = jnp.concatenate(ranks, axis=0).astype(I32)
    carry = carry_ref[...] + jnp.sum(onehot, axis=1, keepdims=True)
    carry_ref[...] = carry
    cnt_ref[...] = carry


def _router(h, g, w_router, b_router, tm):
    s, d = h.shape
    tm = min(tm, s)
    tri = jnp.triu(jnp.ones((tm, tm), BF16), 1)
    kt = lambda: pl.BlockSpec((TOP_K, tm), lambda i: (0, i))
    return pl.pallas_call(
        _router_kernel,
        grid=(s // tm,),
        in_specs=[pl.BlockSpec((tm, d), lambda i: (i, 0)),
                  pl.BlockSpec((1, d), lambda i: (0, 0)),
                  pl.BlockSpec((N_EXPERTS, d), lambda i: (0, 0)),
                  pl.BlockSpec((N_EXPERTS, 1), lambda i: (0, 0)),
                  pl.BlockSpec((tm, tm), lambda i: (0, 0))],
        out_specs=[pl.BlockSpec((tm, d // 2), lambda i: (i, 0)), kt(), kt(), kt(),
                   pl.BlockSpec((N_EXPERTS, 128), lambda i: (0, 0))],
        out_shape=[jax.ShapeDtypeStruct((s, d // 2), U32),
                   jax.ShapeDtypeStruct((TOP_K, s), I32),
                   jax.ShapeDtypeStruct((TOP_K, s), F32),
                   jax.ShapeDtypeStruct((TOP_K, s), I32),
                   jax.ShapeDtypeStruct((N_EXPERTS, 128), F32)],
        scratch_shapes=[pltpu.VMEM((N_EXPERTS, 128), F32)],
        compiler_params=_cparams(("arbitrary",)),
        name="router",
    )(h, g.reshape(1, d), w_router.T, b_router.reshape(N_EXPERTS, 1), tri)


def _experts_kernel(ge_ref, gx_ref, nsub_ref, nrow_ref,
                    x_ref, wg_ref, wu_ref, wd_ref, bg_ref, bu_ref, bd_ref, o_ref, xb_ref):
    g = pl.program_id(0)
    f = pl.program_id(1)
    ns = nsub_ref[g]
    half = x_ref.shape[1]

    def compute(rows, first):
        if first:
            w = x_ref[0:rows, :]
            valid = lax.broadcasted_iota(I32, w.shape, 0) < nrow_ref[g]
            lo = jnp.where(valid, pltpu.bitcast(w << 16, F32), 0.0).astype(BF16)
            hi = jnp.where(valid, pltpu.bitcast(w & jnp.uint32(0xFFFF0000), F32), 0.0).astype(BF16)
            xj = jnp.concatenate([lo, hi], axis=1)
            xb_ref[0:rows, :] = xj
        else:
            xj = xb_ref[0:rows, :]
        gate = jnp.dot(xj, wg_ref[0].astype(BF16), preferred_element_type=F32) + bg_ref[0]
        up = jnp.dot(xj, wu_ref[0].astype(BF16), preferred_element_type=F32) + bu_ref[0]
        gate = jnp.minimum(gate, SWIGLU_LIMIT)
        up = jnp.clip(up, -SWIGLU_LIMIT, SWIGLU_LIMIT)
        act = (up + 1.0) * gate * jax.nn.sigmoid(SWIGLU_ALPHA * gate)
        y = jnp.dot(act.astype(BF16), wd_ref[0].astype(BF16), preferred_element_type=F32)
        if first:
            o_ref[0:rows, :] = y + bd_ref[0]
            if rows < GROUP_ROWS:
                o_ref[rows:, :] = jnp.zeros((GROUP_ROWS - rows, o_ref.shape[1]), F32)
        else:
            o_ref[0:rows, :] += y

    for n in range(1, GROUP_ROWS // SUB_ROWS + 1):
        @pl.when(jnp.logical_and(ns == n, f == 0))
        def _(n=n):
            compute(n * SUB_ROWS, True)

        @pl.when(jnp.logical_and(ns == n, f > 0))
        def _(n=n):
            compute(n * SUB_ROWS, False)

    @pl.when(jnp.logical_and(ns == 0, f == 0))
    def _():
        o_ref[...] = jnp.zeros(o_ref.shape, F32)


def _experts(xs, w_gu, b_gu, w_down, b_down, g_e, g_x, g_nsub, g_nrow):
    m_pad, half = xs.shape
    d = 2 * half
    e, _, ff2 = w_gu.shape
    ff = ff2 // 2
    nf = ff // FF_TILE
    ng = m_pad // GROUP_ROWS

    def ft(g, f, ns):
        return jnp.where(ns[g] > 0, f, nf - 1)

    grid_spec = pltpu.PrefetchScalarGridSpec(
        num_scalar_prefetch=4,
        grid=(ng, nf),
        in_specs=[pl.BlockSpec((GROUP_ROWS, half), lambda g, f, ge, gx, ns, nr: (gx[g], 0)),
                  pl.BlockSpec((1, d, FF_TILE), lambda g, f, ge, gx, ns, nr: (ge[g], 0, ft(g, f, ns))),
                  pl.BlockSpec((1, d, FF_TILE), lambda g, f, ge, gx, ns, nr: (ge[g], 0, nf + ft(g, f, ns))),
                  pl.BlockSpec((1, FF_TILE, d), lambda g, f, ge, gx, ns, nr: (ge[g], ft(g, f, ns), 0)),
                  pl.BlockSpec((1, 1, FF_TILE), lambda g, f, ge, gx, ns, nr: (ge[g], 0, ft(g, f, ns))),
                  pl.BlockSpec((1, 1, FF_TILE), lambda g, f, ge, gx, ns, nr: (ge[g], 0, nf + ft(g, f, ns))),
                  pl.BlockSpec((1, 1, d), lambda g, f, ge, gx, ns, nr: (ge[g], 0, 0))],
        out_specs=pl.BlockSpec((GROUP_ROWS, d), lambda g, f, ge, gx, ns, nr: (g, 0)),
        scratch_shapes=[pltpu.VMEM((GROUP_ROWS, d), BF16)],
    )
    return pl.pallas_call(
        _experts_kernel,
        grid_spec=grid_spec,
        out_shape=jax.ShapeDtypeStruct((m_pad, d), F32),
        compiler_params=_cparams(("arbitrary", "arbitrary"), EXPERTS_VMEM_LIMIT),
        name="experts",
    )(g_e, g_x, g_nsub, g_nrow, xs, w_gu, w_gu, w_down,
      b_gu.reshape(e, 1, ff2), b_gu.reshape(e, 1, ff2), b_down.reshape(e, 1, d))


ROW_DMA_UNROLL = 8


def _for_rows(n_rows, fn):
    def body(r, carry):
        for k in range(TOP_K):
            fn(r, k)
        return carry
    lax.fori_loop(0, n_rows, body, 0, unroll=ROW_DMA_UNROLL)


def _dispatch_kernel(dest_ref, x_ref, xs_ref, sem):
    tm = x_ref.shape[0]

    def row_copy(r, k):
        return pltpu.make_async_copy(x_ref.at[pl.ds(r, 1)], xs_ref.at[pl.ds(dest_ref[k, r], 1)], sem)

    _for_rows(tm, lambda r, k: row_copy(r, k).start())
    _for_rows(tm, lambda r, k: row_copy(r, k).wait())


def _dispatch(xp, dest, m_pad, tm):
    s, w = xp.shape
    tm = min(tm, s)
    return pl.pallas_call(
        _dispatch_kernel,
        grid=(s // tm,),
        in_specs=[pl.BlockSpec((TOP_K, tm), lambda i: (0, i), memory_space=pltpu.SMEM),
                  pl.BlockSpec((tm, w), lambda i: (i, 0))],
        out_specs=pl.BlockSpec(memory_space=pl.ANY),
        out_shape=jax.ShapeDtypeStruct((m_pad, w), U32),
        scratch_shapes=[pltpu.SemaphoreType.DMA],
        compiler_params=_cparams(("arbitrary",)),
        name="dispatch",
    )(dest, xp)


def _combine_kernel(dest_ref, dnext_ref, gate_ref, h_ref, g_ref, y_ref, o_ref, buf_ref, sem):
    i = pl.program_id(0)
    tm = h_ref.shape[0]
    cur = lax.rem(i, 2)

    def row_copy(d_ref, b, r, k):
        return pltpu.make_async_copy(y_ref.at[pl.ds(d_ref[k, r], 1)], buf_ref.at[b, k, pl.ds(r, 1)], sem.at[b])

    @pl.when(i == 0)
    def _():
        _for_rows(tm, lambda r, k: row_copy(dest_ref, 0, r, k).start())

    @pl.when(i + 1 < pl.num_programs(0))
    def _():
        _for_rows(tm, lambda r, k: row_copy(dnext_ref, 1 - cur, r, k).start())

    _for_rows(tm, lambda r, k: row_copy(dest_ref, cur, r, k).wait())
    acc = h_ref[...]
    gates = gate_ref[...]
    for k in range(TOP_K):
        acc = acc + buf_ref[cur, k] * gates[:, k:k + 1]
    o_ref[...] = _rms(acc, g_ref[...], EPS)


def _combine(y, dest, gates_t, h, g, tm):
    s, d = h.shape
    tm = min(tm, s)
    nb = s // tm
    return pl.pallas_call(
        _combine_kernel,
        grid=(nb,),
        in_specs=[pl.BlockSpec((TOP_K, tm), lambda i: (0, i), memory_space=pltpu.SMEM),
                  pl.BlockSpec((TOP_K, tm), lambda i: (0, jnp.minimum(i + 1, nb - 1)), memory_space=pltpu.SMEM),
                  pl.BlockSpec((tm, TOP_K), lambda i: (i, 0)),
                  pl.BlockSpec((tm, d), lambda i: (i, 0)),
                  pl.BlockSpec((1, d), lambda i: (0, 0)),
                  pl.BlockSpec(memory_space=pl.ANY)],
        out_specs=pl.BlockSpec((tm, d), lambda i: (i, 0)),
        out_shape=jax.ShapeDtypeStruct((s, d), F32),
        scratch_shapes=[pltpu.VMEM((2, TOP_K, tm, d), F32), pltpu.SemaphoreType.DMA((2,))],
        compiler_params=_cparams(("arbitrary",)),
        name="combine",
    )(dest, dest, gates_t, h, g.reshape(1, d), y)


def _routing_tables(idx, rank, counts, n_groups):
    seg_groups = (counts + GROUP_ROWS - 1) // GROUP_ROWS
    grp_end = jnp.cumsum(seg_groups)
    grp_start = grp_end - seg_groups
    row_start = grp_start * GROUP_ROWS
    hit = idx[..., None] == jnp.arange(N_EXPERTS, dtype=I32)
    dest = jnp.sum(jnp.where(hit, row_start, 0), axis=-1) + rank
    total = grp_end[-1]
    gid = jnp.arange(n_groups, dtype=I32)
    used = gid < total
    gclamp = jnp.minimum(gid, total - 1)
    g_e = jnp.minimum(jnp.searchsorted(grp_end, gclamp, side="right"), N_EXPERTS - 1).astype(I32)
    rows = jnp.clip(counts[g_e] - (gclamp - grp_start[g_e]) * GROUP_ROWS, 0, GROUP_ROWS)
    g_nrow = jnp.where(used, rows, 0).astype(I32)
    g_nsub = (g_nrow + SUB_ROWS - 1) // SUB_ROWS
    return dest.astype(I32), g_e, gclamp.astype(I32), g_nsub.astype(I32), g_nrow


def kernel(x, mem, norm_mix_g, w_in, conv_w, conv_b, w_rg_a, b_rg_a, w_rg_x, b_rg_x, rg_lambda, rec_norm_g,
           lambda_q1, lambda_k1, lambda_q2, lambda_k2, subln_g, w_out, norm_cross_g, norm_mem_g, w_cq, w_ckv,
           w_co, norm_ffn_g, w_router, b_router, w_gate_up, b_gate_up, w_down, b_down, norm_final_g):
    b, s, d = x.shape
    assert b == 1 and w_in.shape[0] == 1
    aw = DA_HEADS * 2 * DA_HEAD_DIM
    h0 = x.reshape(s, d)

    w_in_b = w_in[0].astype(BF16)
    ones = lambda n: jnp.ones((1, n), F32)
    q_scale = jnp.concatenate([jnp.full((1, aw), (DA_HEAD_DIM ** -0.5) * LOG2E, F32), ones(2 * aw)], axis=1)
    qkv, rg = _in_proj(h0, norm_mix_g[0], w_in_b, q_scale, 3 * aw, 1024, 1024)
    lam_params = jnp.stack([lambda_q1[0], lambda_k1[0], lambda_q2[0], lambda_k2[0]])
    attn = _diff_attn(qkv, lam_params, subln_g[0], 2048, 8)
    rec = _rglru(rg, conv_w[0], conv_b[0], w_rg_a[0], b_rg_a[0], w_rg_x[0], b_rg_x[0],
                 rg_lambda[0], rec_norm_g[0], 256)
    h1 = _matmul_res(attn, 0, rec, 0, w_out[0].astype(BF16), h0, 1024, 1024)

    nm = mem.shape[1]
    kv = _norm_matmul(mem.reshape(nm, d), norm_mem_g[0], w_ckv[0].astype(BF16), ones(2 * d), BF16, 256, 1024)
    o = _cross_attn(h1, norm_cross_g[0], w_cq[0].astype(BF16), kv, 512)
    h2 = _matmul_res(o, 0, o, 1, w_co[0].astype(BF16), h1, 1024, 1024)

    xp, idx, gates, rank, cnt = _router(h2, norm_ffn_g[0], w_router[0], b_router[0], 512)
    n_groups = (s * TOP_K) // GROUP_ROWS + N_EXPERTS
    dest, g_e, g_x, g_nsub, g_nrow = _routing_tables(idx, rank, cnt[:, 0].astype(I32), n_groups)
    xs = _dispatch(xp, dest, n_groups * GROUP_ROWS, 256)
    y = _experts(xs, w_gate_up[0], b_gate_up[0], w_down[0], b_down[0], g_e, g_x, g_nsub, g_nrow)
    out = _combine(y, dest, gates.T, h2, norm_final_g, 128)
    return out.reshape(b, s, d)
```

```python
import functools
import math

import jax
import jax.numpy as jnp
from jax import lax
from jax.experimental import pallas as pl
from jax.experimental.pallas import tpu as pltpu

F32 = jnp.float32
BF16 = jnp.bfloat16
U32 = jnp.uint32
I32 = jnp.int32

DA_HEAD_DIM = 128
DA_HEADS = 4
REC_BLOCKS = 8
CONV_W = 4
RG_C = 8.0
CROSS_HEADS = 4
N_EXPERTS = 32
TOP_K = 4
SWIGLU_LIMIT = 7.0
SWIGLU_ALPHA = 1.702
EPS = 1e-6
DA_EPS = 1e-5
LAMBDA_INIT = 0.8 - 0.6 * math.exp(-0.3 * 0)
LOG2E = 1.4426950408889634

V7X_VMEM_BYTES = 64 * 1024 * 1024
VMEM_LIMIT = V7X_VMEM_BYTES - 8 * 1024 * 1024
EXPERTS_VMEM_LIMIT = V7X_VMEM_BYTES - 4 * 1024 * 1024

GROUP_ROWS = 1024
SUB_ROWS = 256
FF_TILE = 512


def _cparams(sem, vmem_limit=VMEM_LIMIT):
    return pltpu.CompilerParams(dimension_semantics=sem, vmem_limit_bytes=vmem_limit)


def _rms(x, g, eps):
    return x * lax.rsqrt(jnp.mean(x * x, axis=-1, keepdims=True) + eps) * g


def _norm_matmul_kernel(x_ref, g_ref, w_ref, cs_ref, o_ref, xn_ref):
    @pl.when(pl.program_id(1) == 0)
    def _():
        xn_ref[...] = _rms(x_ref[...], g_ref[...], EPS).astype(BF16)

    acc = jnp.dot(xn_ref[...], w_ref[...], preferred_element_type=F32)
    o_ref[...] = (acc * cs_ref[...]).astype(o_ref.dtype)


def _norm_matmul(x, g, w, col_scale, out_dtype, tm, tn):
    m, k = x.shape
    n = w.shape[1]
    tm, tn = min(tm, m), min(tn, n)
    return pl.pallas_call(
        _norm_matmul_kernel,
        grid=(m // tm, n // tn),
        in_specs=[pl.BlockSpec((tm, k), lambda i, j: (i, 0)),
                  pl.BlockSpec((1, k), lambda i, j: (0, 0)),
                  pl.BlockSpec((k, tn), lambda i, j: (0, j)),
                  pl.BlockSpec((1, tn), lambda i, j: (0, j))],
        out_specs=pl.BlockSpec((tm, tn), lambda i, j: (i, j)),
        out_shape=jax.ShapeDtypeStruct((m, n), out_dtype),
        scratch_shapes=[pltpu.VMEM((tm, k), BF16)],
        compiler_params=_cparams(("parallel", "arbitrary")),
        name="norm_matmul",
    )(x, g.reshape(1, k), w, col_scale)


def _in_proj_kernel(x_ref, g_ref, w_ref, cs_ref, qkv_ref, rg_ref, xn_ref, *, n_qkv_tiles):
    j = pl.program_id(1)

    @pl.when(j == 0)
    def _():
        xn_ref[...] = _rms(x_ref[...], g_ref[...], EPS).astype(BF16)

    hw = w_ref.shape[1] // 2

    @pl.when(j < n_qkv_tiles)
    def _():
        for c in range(2):
            acc = jnp.dot(xn_ref[...], w_ref[:, c * hw:(c + 1) * hw], preferred_element_type=F32)
            qkv_ref[:, c * hw:(c + 1) * hw] = (acc * cs_ref[:, c * hw:(c + 1) * hw]).astype(qkv_ref.dtype)

    @pl.when(j >= n_qkv_tiles)
    def _():
        for c in range(2):
            rg_ref[:, c * hw:(c + 1) * hw] = jnp.dot(xn_ref[...], w_ref[:, c * hw:(c + 1) * hw],
                                                     preferred_element_type=F32)


def _in_proj(x, g, w, q_scale, n_qkv, tm, tn):
    m, k = x.shape
    n = w.shape[1]
    tm = min(tm, m)
    assert n_qkv % tn == 0 and (n - n_qkv) % tn == 0
    nq = n_qkv // tn
    return pl.pallas_call(
        functools.partial(_in_proj_kernel, n_qkv_tiles=nq),
        grid=(m // tm, n // tn),
        in_specs=[pl.BlockSpec((tm, k), lambda i, j: (i, 0)),
                  pl.BlockSpec((1, k), lambda i, j: (0, 0)),
                  pl.BlockSpec((k, tn), lambda i, j: (0, j)),
                  pl.BlockSpec((1, tn), lambda i, j: (0, jnp.minimum(j, nq - 1)))],
        out_specs=[pl.BlockSpec((tm, tn), lambda i, j: (i, jnp.minimum(j, nq - 1))),
                   pl.BlockSpec((tm, tn), lambda i, j: (i, jnp.maximum(j - nq, 0)))],
        out_shape=[jax.ShapeDtypeStruct((m, n_qkv), BF16), jax.ShapeDtypeStruct((m, n - n_qkv), F32)],
        scratch_shapes=[pltpu.VMEM((tm, k), BF16)],
        compiler_params=_cparams(("parallel", "arbitrary")),
        name="in_proj",
    )(x, g.reshape(1, k), w, q_scale)


def _matmul_res_kernel(a0_ref, a1_ref, w0_ref, w1_ref, r_ref, o_ref):
    hw = o_ref.shape[1] // 2
    for c in range(2):
        cols = slice(c * hw, (c + 1) * hw)
        acc = jnp.dot(a0_ref[...], w0_ref[:, cols], preferred_element_type=F32)
        acc += jnp.dot(a1_ref[...], w1_ref[:, cols], preferred_element_type=F32)
        o_ref[:, cols] = r_ref[:, cols] + acc


def _matmul_res(a0, c0, a1, c1, w, res, tm, tn):
    m, n = res.shape
    kh = w.shape[0] // 2
    tm, tn = min(tm, m), min(tn, n)
    w_mode = pl.Buffered(1) if tn == n else None
    return pl.pallas_call(
        _matmul_res_kernel,
        grid=(m // tm, n // tn),
        in_specs=[pl.BlockSpec((tm, kh), lambda i, j: (i, c0)),
                  pl.BlockSpec((tm, kh), lambda i, j: (i, c1)),
                  pl.BlockSpec((kh, tn), lambda i, j: (0, j), pipeline_mode=w_mode),
                  pl.BlockSpec((kh, tn), lambda i, j: (1, j), pipeline_mode=w_mode),
                  pl.BlockSpec((tm, tn), lambda i, j: (i, j))],
        out_specs=pl.BlockSpec((tm, tn), lambda i, j: (i, j)),
        out_shape=jax.ShapeDtypeStruct((m, n), F32),
        compiler_params=_cparams(("parallel", "parallel")),
        name="matmul_res",
    )(a0, a1, w, w, res)


def _diff_attn_kernel(qi_ref, ki_ref, q_ref, k_ref, v_ref, lam_ref, g_ref, o_ref,
                      m_ref, l_ref, acc_ref, *, t, n_split):
    p = pl.program_id(1)
    qi = qi_ref[p]
    ki = ki_ref[p]
    d = DA_HEAD_DIM
    rs = t // n_split

    @pl.when(ki == 0)
    def _():
        m_ref[...] = jnp.full(m_ref.shape, -jnp.inf, F32)
        l_ref[...] = jnp.zeros(l_ref.shape, F32)
        acc_ref[...] = jnp.zeros(acc_ref.shape, F32)

    def step(diag):
        for h in range(n_split):
            r0 = h * rs
            kw = r0 + rs if diag else t
            v = v_ref[0:kw, :]
            for i in range(2):
                q = q_ref[r0:r0 + rs, i * d:(i + 1) * d]
                k = k_ref[0:kw, i * d:(i + 1) * d]
                s = lax.dot_general(q, k, (((1,), (1,)), ((), ())), preferred_element_type=F32)
                if diag:
                    row = r0 + lax.broadcasted_iota(I32, (rs, kw), 0)
                    col = lax.broadcasted_iota(I32, (rs, kw), 1)
                    s = jnp.where(col <= row, s, -jnp.inf)
                m_old = m_ref[i, r0:r0 + rs, :1]
                m_new = jnp.maximum(m_old, jnp.max(s, axis=-1, keepdims=True))
                alpha = jnp.exp2(m_old - m_new)
                pr = jnp.exp2(s - m_new)
                l_ref[i, r0:r0 + rs, :] = jnp.broadcast_to(
                    alpha * l_ref[i, r0:r0 + rs, :1] + jnp.sum(pr, axis=-1, keepdims=True), (rs, 128))
                m_ref[i, r0:r0 + rs, :] = jnp.broadcast_to(m_new, (rs, 128))
                acc_ref[i, r0:r0 + rs, :] = alpha * acc_ref[i, r0:r0 + rs, :] + jnp.dot(
                    pr.astype(BF16), v, preferred_element_type=F32)

    @pl.when(ki == qi)
    def _():
        step(True)

    @pl.when(ki < qi)
    def _():
        step(False)

    @pl.when(ki == qi)
    def _():
        lp = lam_ref[...]
        lam = (jnp.exp(jnp.sum(lp[0:1] * lp[1:2], axis=-1, keepdims=True))
               - jnp.exp(jnp.sum(lp[2:3] * lp[3:4], axis=-1, keepdims=True)) + LAMBDA_INIT)
        o = acc_ref[0] / l_ref[0][:, :1] - lam * (acc_ref[1] / l_ref[1][:, :1])
        o = _rms(o, g_ref[...], DA_EPS) * (1.0 - LAMBDA_INIT)
        o_ref[...] = o.astype(o_ref.dtype)


def _diff_attn(qkv, lam_params, subln_g, t, n_split):
    s = qkv.shape[0]
    dv = 2 * DA_HEAD_DIM
    t = min(t, s)
    n_split = min(n_split, t // 128)
    assert s % t == 0 and t % n_split == 0
    nq = s // t
    pairs = [(a, b) for a in range(nq) for b in range(a + 1)]
    qi = jnp.asarray([a for a, _ in pairs], I32)
    ki = jnp.asarray([b for _, b in pairs], I32)
    grid_spec = pltpu.PrefetchScalarGridSpec(
        num_scalar_prefetch=2,
        grid=(DA_HEADS, len(pairs)),
        in_specs=[pl.BlockSpec((t, dv), lambda h, p, qi, ki: (qi[p], h)),
                  pl.BlockSpec((t, dv), lambda h, p, qi, ki: (ki[p], DA_HEADS + h)),
                  pl.BlockSpec((t, dv), lambda h, p, qi, ki: (ki[p], 2 * DA_HEADS + h)),
                  pl.BlockSpec((4, DA_HEAD_DIM), lambda h, p, qi, ki: (0, 0)),
                  pl.BlockSpec((1, dv), lambda h, p, qi, ki: (0, 0))],
        out_specs=pl.BlockSpec((t, dv), lambda h, p, qi, ki: (qi[p], h)),
        scratch_shapes=[pltpu.VMEM((2, t, 128), F32),
                        pltpu.VMEM((2, t, 128), F32),
                        pltpu.VMEM((2, t, dv), F32)],
    )
    return pl.pallas_call(
        functools.partial(_diff_attn_kernel, t=t, n_split=n_split),
        grid_spec=grid_spec,
        out_shape=jax.ShapeDtypeStruct((s, DA_HEADS * dv), BF16),
        compiler_params=_cparams(("parallel", "arbitrary")),
        name="diff_attn",
    )(qi, ki, qkv, qkv, qkv, lam_params, subln_g.reshape(1, dv))


def _rglru_kernel(xr_ref, xg_ref, cw_ref, cb_ref, wa_ref, ba_ref, wx_ref, bx_ref, lam_ref, g_ref,
                  o_ref, ext_ref, h_ref, *, t):
    @pl.when(pl.program_id(0) == 0)
    def _():
        ext_ref[0:8, :] = jnp.zeros((8, ext_ref.shape[1]), F32)
        h_ref[...] = jnp.zeros(h_ref.shape, F32)

    xr = xr_ref[...]
    c = xr.shape[1]
    bd = c // REC_BLOCKS
    ext_ref[8:8 + t, :] = xr
    cw = cw_ref[...]
    xc = xr * cw[CONV_W - 1:CONV_W] + cb_ref[...]
    for dlt in range(1, CONV_W):
        xc = xc + ext_ref[8 - dlt:8 - dlt + t, :] * cw[CONV_W - 1 - dlt:CONV_W - dlt]
    ext_ref[0:8, :] = xr[t - 8:t]

    xcb = xc.astype(BF16)
    ra, ri = [], []
    for n in range(REC_BLOCKS):
        xb = xcb[:, n * bd:(n + 1) * bd]
        ra.append(jnp.dot(xb, wa_ref[n], preferred_element_type=F32))
        ri.append(jnp.dot(xb, wx_ref[n], preferred_element_type=F32))
    r = jax.nn.sigmoid(jnp.concatenate(ra, axis=1) + ba_ref[...])
    ig = jax.nn.sigmoid(jnp.concatenate(ri, axis=1) + bx_ref[...])
    log_a = (-RG_C * jax.nn.softplus(-lam_ref[...])) * r
    a = jnp.exp(log_a)
    u = -jnp.tanh(log_a) * (a * a + 1.0)
    root = jnp.where(u > 0.0, u * lax.rsqrt(u), 0.0)
    b = root * (ig * xc)

    a = a.reshape(t // 8, 8, c)
    b = b.reshape(t // 8, 8, c)
    sub = lax.broadcasted_iota(I32, a.shape, 1)
    for dlt in (1, 2, 4):
        keep = sub < dlt
        a_sh = jnp.where(keep, 1.0, pltpu.roll(a, dlt, axis=1))
        b_sh = jnp.where(keep, 0.0, pltpu.roll(b, dlt, axis=1))
        b = a * b_sh + b
        a = a * a_sh
    h_in = h_ref[...]
    hs = []
    for j in range(t // 8):
        hj = a[j] * h_in + b[j]
        hs.append(hj)
        h_in = jnp.broadcast_to(hj[7:8], hj.shape)
    h = jnp.concatenate(hs, axis=0)
    h_ref[...] = h_in

    y = h * jax.nn.gelu(xg_ref[...])
    o_ref[...] = _rms(y, g_ref[...], EPS).astype(o_ref.dtype)


def _rglru(rg, conv_w, conv_b, w_a, b_a, w_x, b_x, rg_lambda, rec_norm_g, t):
    s = rg.shape[0]
    c = rg.shape[1] // 2
    t = min(t, s)
    bd = c // REC_BLOCKS
    vec = lambda: pl.BlockSpec((1, c), lambda i: (0, 0))
    return pl.pallas_call(
        functools.partial(_rglru_kernel, t=t),
        grid=(s // t,),
        in_specs=[pl.BlockSpec((t, c), lambda i: (i, 0)),
                  pl.BlockSpec((t, c), lambda i: (i, 1)),
                  pl.BlockSpec((CONV_W, c), lambda i: (0, 0)),
                  vec(),
                  pl.BlockSpec((REC_BLOCKS, bd, bd), lambda i: (0, 0, 0)),
                  vec(),
                  pl.BlockSpec((REC_BLOCKS, bd, bd), lambda i: (0, 0, 0)),
                  vec(), vec(), vec()],
        out_specs=pl.BlockSpec((t, c), lambda i: (i, 0)),
        out_shape=jax.ShapeDtypeStruct((s, c), BF16),
        scratch_shapes=[pltpu.VMEM((t + 8, c), F32), pltpu.VMEM((8, c), F32)],
        compiler_params=_cparams(("arbitrary",)),
        name="rglru",
    )(rg, rg, conv_w, conv_b.reshape(1, c), w_a.astype(BF16), b_a.reshape(1, c),
      w_x.astype(BF16), b_x.reshape(1, c), rg_lambda.reshape(1, c), rec_norm_g.reshape(1, c))


def _cross_attn_kernel(h_ref, g_ref, wq_ref, kv_ref, o_ref):
    d = h_ref.shape[1]
    hd = d // CROSS_HEADS
    c = (hd ** -0.5) * LOG2E
    n = _rms(h_ref[...], g_ref[...], EPS).astype(BF16)
    q = jnp.dot(n, wq_ref[...], preferred_element_type=F32).astype(BF16)
    for hh in range(CROSS_HEADS):
        qh = q[:, hh * hd:(hh + 1) * hd]
        kh = kv_ref[:, hh * hd:(hh + 1) * hd]
        vh = kv_ref[:, d + hh * hd:d + (hh + 1) * hd]
        s = lax.dot_general(qh, kh, (((1,), (1,)), ((), ())), preferred_element_type=F32) * c
        pr = jnp.exp2(s - jnp.max(s, axis=-1, keepdims=True))
        l = jnp.sum(pr, axis=-1, keepdims=True)
        o = jnp.dot(pr.astype(BF16), vh, preferred_element_type=F32) / l
        o_ref[:, hh * hd:(hh + 1) * hd] = o.astype(o_ref.dtype)


def _cross_attn(h, g, wq, kv, tm):
    s, d = h.shape
    nm = kv.shape[0]
    tm = min(tm, s)
    return pl.pallas_call(
        _cross_attn_kernel,
        grid=(s // tm,),
        in_specs=[pl.BlockSpec((tm, d), lambda i: (i, 0)),
                  pl.BlockSpec((1, d), lambda i: (0, 0)),
                  pl.BlockSpec((d, d), lambda i: (0, 0)),
                  pl.BlockSpec((nm, 2 * d), lambda i: (0, 0))],
        out_specs=pl.BlockSpec((tm, d), lambda i: (i, 0)),
        out_shape=jax.ShapeDtypeStruct((s, d), BF16),
        compiler_params=_cparams(("parallel",)),
        name="cross_attn",
    )(h, g.reshape(1, d), wq, kv)


def _router_kernel(h_ref, g_ref, wrt_ref, br_ref, tri_ref,
                   xp_ref, idx_ref, gate_ref, rank_ref, cnt_ref, carry_ref):
    @pl.when(pl.program_id(0) == 0)
    def _():
        carry_ref[...] = jnp.zeros(carry_ref.shape, F32)

    tm, d = h_ref.shape
    half = d // 2
    hn = _rms(h_ref[...], g_ref[...], EPS)
    bits = pltpu.bitcast(hn.astype(BF16).astype(F32), U32)
    xp_ref[...] = (bits[:, :half] >> 16) | (bits[:, half:] & jnp.uint32(0xFFFF0000))

    logits = lax.dot_general(wrt_ref[...], hn, (((1,), (1,)), ((), ())),
                             precision=lax.Precision.HIGHEST,
                             preferred_element_type=F32) + br_ref[...]
    e_iota = lax.broadcasted_iota(I32, logits.shape, 0)
    vals, idxs, hits = [], [], []
    l = logits
    for _ in range(TOP_K):
        mx = jnp.max(l, axis=0, keepdims=True)
        ik = jnp.min(jnp.where(l == mx, e_iota, N_EXPERTS), axis=0, keepdims=True)
        hit = e_iota == ik
        vals.append(mx)
        idxs.append(ik)
        hits.append(hit)
        l = jnp.where(hit, -jnp.inf, l)
    ex = [jnp.exp(v - vals[0]) for v in vals]
    den = ex[0] + ex[1] + ex[2] + ex[3]
    gate_ref[...] = jnp.concatenate([e / den for e in ex], axis=0)
    idx_ref[...] = jnp.concatenate(idxs, axis=0)

    onehot = jnp.zeros(logits.shape, F32)
    for hit in hits:
        onehot = onehot + jnp.where(hit, 1.0, 0.0)
    before = jnp.dot(onehot.astype(BF16), tri_ref[...], preferred_element_type=F32)
    before = before + carry_ref[:, :1]
    ranks = [jnp.sum(jnp.where(hit, before, 0.0), axis=0, keepdims=True) for hit in hits]
    rank_ref[...] = jnp.concatenate(ranks, axis=0).astype(I32)
    carry = carry_ref[...] + jnp.sum(onehot, axis=1, keepdims=True)
    carry_ref[...] = carry
    cnt_ref[...] = carry


def _router(h, g, w_router, b_router, tm):
    s, d = h.shape
    tm = min(tm, s)
    tri = jnp.triu(jnp.ones((tm, tm), BF16), 1)
    kt = lambda: pl.BlockSpec((TOP_K, tm), lambda i: (0, i))
    return pl.pallas_call(
        _router_kernel,
        grid=(s // tm,),
        in_specs=[pl.BlockSpec((tm, d), lambda i: (i, 0)),
                  pl.BlockSpec((1, d), lambda i: (0, 0)),
                  pl.BlockSpec((N_EXPERTS, d), lambda i: (0, 0)),
                  pl.BlockSpec((N_EXPERTS, 1), lambda i: (0, 0)),
                  pl.BlockSpec((tm, tm), lambda i: (0, 0))],
        out_specs=[pl.BlockSpec((tm, d // 2), lambda i: (i, 0)), kt(), kt(), kt(),
                   pl.BlockSpec((N_EXPERTS, 128), lambda i: (0, 0))],
        out_shape=[jax.ShapeDtypeStruct((s, d // 2), U32),
                   jax.ShapeDtypeStruct((TOP_K, s), I32),
                   jax.ShapeDtypeStruct((TOP_K, s), F32),
                   jax.ShapeDtypeStruct((TOP_K, s), I32),
                   jax.ShapeDtypeStruct((N_EXPERTS, 128), F32)],
        scratch_shapes=[pltpu.VMEM((N_EXPERTS, 128), F32)],
        compiler_params=_cparams(("arbitrary",)),
        name="router",
    )(h, g.reshape(1, d), w_router.T, b_router.reshape(N_EXPERTS, 1), tri)


def _experts_kernel(ge_ref, gx_ref, nsub_ref, nrow_ref,
                    x_ref, wg_ref, wu_ref, wd_ref, bg_ref, bu_ref, bd_ref, o_ref, xb_ref):
    g = pl.program_id(0)
    f = pl.program_id(1)
    ns = nsub_ref[g]
    half = x_ref.shape[1]

    def compute(rows, first):
        if first:
            w = x_ref[0:rows, :]
            valid = lax.broadcasted_iota(I32, w.shape, 0) < nrow_ref[g]
            lo = jnp.where(valid, pltpu.bitcast(w << 16, F32), 0.0).astype(BF16)
            hi = jnp.where(valid, pltpu.bitcast(w & jnp.uint32(0xFFFF0000), F32), 0.0).astype(BF16)
            xj = jnp.concatenate([lo, hi], axis=1)
            xb_ref[0:rows, :] = xj
        else:
            xj = xb_ref[0:rows, :]
        gate = jnp.dot(xj, wg_ref[0].astype(BF16), preferred_element_type=F32) + bg_ref[0]
        up = jnp.dot(xj, wu_ref[0].astype(BF16), preferred_element_type=F32) + bu_ref[0]
        gate = jnp.minimum(gate, SWIGLU_LIMIT)
        up = jnp.clip(up, -SWIGLU_LIMIT, SWIGLU_LIMIT)
        act = (up + 1.0) * gate * jax.nn.sigmoid(SWIGLU_ALPHA * gate)
        y = jnp.dot(act.astype(BF16), wd_ref[0].astype(BF16), preferred_element_type=F32)
        if first:
            o_ref[0:rows, :] = y + bd_ref[0]
            if rows < GROUP_ROWS:
                o_ref[rows:, :] = jnp.zeros((GROUP_ROWS - rows, o_ref.shape[1]), F32)
        else:
            o_ref[0:rows, :] += y

    for n in range(1, GROUP_ROWS // SUB_ROWS + 1):
        @pl.when(jnp.logical_and(ns == n, f == 0))
        def _(n=n):
            compute(n * SUB_ROWS, True)

        @pl.when(jnp.logical_and(ns == n, f > 0))
        def _(n=n):
            compute(n * SUB_ROWS, False)

    @pl.when(jnp.logical_and(ns == 0, f == 0))
    def _():
        o_ref[...] = jnp.zeros(o_ref.shape, F32)


def _experts(xs, w_gu, b_gu, w_down, b_down, g_e, g_x, g_nsub, g_nrow):
    m_pad, half = xs.shape
    d = 2 * half
    e, _, ff2 = w_gu.shape
    ff = ff2 // 2
    nf = ff // FF_TILE
    ng = m_pad // GROUP_ROWS

    def ft(g, f, ns):
        return jnp.where(ns[g] > 0, f, nf - 1)

    grid_spec = pltpu.PrefetchScalarGridSpec(
        num_scalar_prefetch=4,
        grid=(ng, nf),
        in_specs=[pl.BlockSpec((GROUP_ROWS, half), lambda g, f, ge, gx, ns, nr: (gx[g], 0)),
                  pl.BlockSpec((1, d, FF_TILE), lambda g, f, ge, gx, ns, nr: (ge[g], 0, ft(g, f, ns))),
                  pl.BlockSpec((1, d, FF_TILE), lambda g, f, ge, gx, ns, nr: (ge[g], 0, nf + ft(g, f, ns))),
                  pl.BlockSpec((1, FF_TILE, d), lambda g, f, ge, gx, ns, nr: (ge[g], ft(g, f, ns), 0)),
                  pl.BlockSpec((1, 1, FF_TILE), lambda g, f, ge, gx, ns, nr: (ge[g], 0, ft(g, f, ns))),
                  pl.BlockSpec((1, 1, FF_TILE), lambda g, f, ge, gx, ns, nr: (ge[g], 0, nf + ft(g, f, ns))),
                  pl.BlockSpec((1, 1, d), lambda g, f, ge, gx, ns, nr: (ge[g], 0, 0))],
        out_specs=pl.BlockSpec((GROUP_ROWS, d), lambda g, f, ge, gx, ns, nr: (g, 0)),
        scratch_shapes=[pltpu.VMEM((GROUP_ROWS, d), BF16)],
    )
    return pl.pallas_call(
        _experts_kernel,
        grid_spec=grid_spec,
        out_shape=jax.ShapeDtypeStruct((m_pad, d), F32),
        compiler_params=_cparams(("arbitrary", "arbitrary"), EXPERTS_VMEM_LIMIT),
        name="experts",
    )(g_e, g_x, g_nsub, g_nrow, xs, w_gu, w_gu, w_down,
      b_gu.reshape(e, 1, ff2), b_gu.reshape(e, 1, ff2), b_down.reshape(e, 1, d))


ROW_DMA_UNROLL = 8


def _for_rows(n_rows, fn):
    def body(r, carry):
        for k in range(TOP_K):
            fn(r, k)
        return carry
    lax.fori_loop(0, n_rows, body, 0, unroll=ROW_DMA_UNROLL)


ZERO_ROWS = 128


def _dispatch_kernel(dest_ref, nrow_ref, x_ref, xs_ref, zero_ref, sem, zsem):
    tm = x_ref.shape[0]

    @pl.when(pl.program_id(0) == 0)
    def _():
        zero_ref[...] = jnp.zeros(zero_ref.shape, U32)

        def chunk_copy(g, j):
            r0 = pl.multiple_of(g * GROUP_ROWS + j * ZERO_ROWS, ZERO_ROWS)
            return pltpu.make_async_copy(zero_ref, xs_ref.at[pl.ds(r0, ZERO_ROWS)], zsem)

        def for_chunks(fn):
            def body(g, carry):
                nr = nrow_ref[g]
                for j in range(GROUP_ROWS // ZERO_ROWS):
                    @pl.when(jnp.logical_and(nr > 0, (j + 1) * ZERO_ROWS > nr))
                    def _(j=j):
                        fn(g, j)
                return carry
            lax.fori_loop(0, nrow_ref.shape[0], body, 0)

        for_chunks(lambda g, j: chunk_copy(g, j).start())
        for_chunks(lambda g, j: chunk_copy(g, j).wait())

    def row_copy(r, k):
        return pltpu.make_async_copy(x_ref.at[pl.ds(r, 1)], xs_ref.at[pl.ds(dest_ref[k, r], 1)], sem)

    _for_rows(tm, lambda r, k: row_copy(r, k).start())
    _for_rows(tm, lambda r, k: row_copy(r, k).wait())


def _dispatch(xp, dest, g_nrow, m_pad, tm):
    s, w = xp.shape
    tm = min(tm, s)
    return pl.pallas_call(
        _dispatch_kernel,
        grid=(s // tm,),
        in_specs=[pl.BlockSpec((TOP_K, tm), lambda i: (0, i), memory_space=pltpu.SMEM),
                  pl.BlockSpec(memory_space=pltpu.SMEM),
                  pl.BlockSpec((tm, w), lambda i: (i, 0))],
        out_specs=pl.BlockSpec(memory_space=pl.ANY),
        out_shape=jax.ShapeDtypeStruct((m_pad, w), U32),
        scratch_shapes=[pltpu.VMEM((ZERO_ROWS, w), U32), pltpu.SemaphoreType.DMA, pltpu.SemaphoreType.DMA],
        compiler_params=_cparams(("arbitrary",)),
        name="dispatch",
    )(dest, g_nrow, xp)


def _combine_kernel(dest_ref, dnext_ref, gate_ref, h_ref, g_ref, y_ref, o_ref, buf_ref, sem):
    i = pl.program_id(0)
    tm = h_ref.shape[0]
    cur = lax.rem(i, 2)

    def row_copy(d_ref, b, r, k):
        return pltpu.make_async_copy(y_ref.at[pl.ds(d_ref[k, r], 1)], buf_ref.at[b, k, pl.ds(r, 1)], sem.at[b])

    @pl.when(i == 0)
    def _():
        _for_rows(tm, lambda r, k: row_copy(dest_ref, 0, r, k).start())

    @pl.when(i + 1 < pl.num_programs(0))
    def _():
        _for_rows(tm, lambda r, k: row_copy(dnext_ref, 1 - cur, r, k).start())

    _for_rows(tm, lambda r, k: row_copy(dest_ref, cur, r, k).wait())
    acc = h_ref[...]
    gates = gate_ref[...]
    for k in range(TOP_K):
        acc = acc + buf_ref[cur, k] * gates[:, k:k + 1]
    o_ref[...] = _rms(acc, g_ref[...], EPS)


def _combine(y, dest, gates_t, h, g, tm):
    s, d = h.shape
    tm = min(tm, s)
    nb = s // tm
    return pl.pallas_call(
        _combine_kernel,
        grid=(nb,),
        in_specs=[pl.BlockSpec((TOP_K, tm), lambda i: (0, i), memory_space=pltpu.SMEM),
                  pl.BlockSpec((TOP_K, tm), lambda i: (0, jnp.minimum(i + 1, nb - 1)), memory_space=pltpu.SMEM),
                  pl.BlockSpec((tm, TOP_K), lambda i: (i, 0)),
                  pl.BlockSpec((tm, d), lambda i: (i, 0)),
                  pl.BlockSpec((1, d), lambda i: (0, 0)),
                  pl.BlockSpec(memory_space=pl.ANY)],
        out_specs=pl.BlockSpec((tm, d), lambda i: (i, 0)),
        out_shape=jax.ShapeDtypeStruct((s, d), F32),
        scratch_shapes=[pltpu.VMEM((2, TOP_K, tm, d), F32), pltpu.SemaphoreType.DMA((2,))],
        compiler_params=_cparams(("arbitrary",)),
        name="combine",
    )(dest, dest, gates_t, h, g.reshape(1, d), y)


def _routing_tables(idx, rank, counts, n_groups):
    seg_groups = (counts + GROUP_ROWS - 1) // GROUP_ROWS
    grp_end = jnp.cumsum(seg_groups)
    grp_start = grp_end - seg_groups
    row_start = grp_start * GROUP_ROWS
    hit = idx[..., None] == jnp.arange(N_EXPERTS, dtype=I32)
    dest = jnp.sum(jnp.where(hit, row_start, 0), axis=-1) + rank
    total = grp_end[-1]
    gid = jnp.arange(n_groups, dtype=I32)
    used = gid < total
    gclamp = jnp.minimum(gid, total - 1)
    g_e = jnp.minimum(jnp.searchsorted(grp_end, gclamp, side="right"), N_EXPERTS - 1).astype(I32)
    rows = jnp.clip(counts[g_e] - (gclamp - grp_start[g_e]) * GROUP_ROWS, 0, GROUP_ROWS)
    g_nrow = jnp.where(used, rows, 0).astype(I32)
    g_nsub = (g_nrow + SUB_ROWS - 1) // SUB_ROWS
    return dest.astype(I32), g_e, gclamp.astype(I32), g_nsub.astype(I32), g_nrow


def kernel(x, mem, norm_mix_g, w_in, conv_w, conv_b, w_rg_a, b_rg_a, w_rg_x, b_rg_x, rg_lambda, rec_norm_g,
           lambda_q1, lambda_k1, lambda_q2, lambda_k2, subln_g, w_out, norm_cross_g, norm_mem_g, w_cq, w_ckv,
           w_co, norm_ffn_g, w_router, b_router, w_gate_up, b_gate_up, w_down, b_down, norm_final_g):
    b, s, d = x.shape
    assert b == 1 and w_in.shape[0] == 1
    aw = DA_HEADS * 2 * DA_HEAD_DIM
    h0 = x.reshape(s, d)

    w_in_b = w_in[0].astype(BF16)
    ones = lambda n: jnp.ones((1, n), F32)
    q_scale = jnp.concatenate([jnp.full((1, aw), (DA_HEAD_DIM ** -0.5) * LOG2E, F32), ones(2 * aw)], axis=1)
    qkv, rg = _in_proj(h0, norm_mix_g[0], w_in_b, q_scale, 3 * aw, 1024, 1024)
    lam_params = jnp.stack([lambda_q1[0], lambda_k1[0], lambda_q2[0], lambda_k2[0]])
    attn = _diff_attn(qkv, lam_params, subln_g[0], 2048, 8)
    rec = _rglru(rg, conv_w[0], conv_b[0], w_rg_a[0], b_rg_a[0], w_rg_x[0], b_rg_x[0],
                 rg_lambda[0], rec_norm_g[0], 256)
    h1 = _matmul_res(attn, 0, rec, 0, w_out[0].astype(BF16), h0, 512, 2048)

    nm = mem.shape[1]
    kv = _norm_matmul(mem.reshape(nm, d), norm_mem_g[0], w_ckv[0].astype(BF16), ones(2 * d), BF16, 256, 1024)
    o = _cross_attn(h1, norm_cross_g[0], w_cq[0].astype(BF16), kv, 512)
    h2 = _matmul_res(o, 0, o, 1, w_co[0].astype(BF16), h1, 512, 2048)

    xp, idx, gates, rank, cnt = _router(h2, norm_ffn_g[0], w_router[0], b_router[0], 512)
    n_groups = (s * TOP_K) // GROUP_ROWS + N_EXPERTS
    dest, g_e, g_x, g_nsub, g_nrow = _routing_tables(idx, rank, cnt[:, 0].astype(I32), n_groups)
    xs = _dispatch(xp, dest, g_nrow, n_groups * GROUP_ROWS, 512)
    y = _experts(xs, w_gate_up[0], b_gate_up[0], w_down[0], b_down[0], g_e, g_x, g_nsub, g_nrow)
    out = _combine(y, dest, gates.T, h2, norm_final_g, 256)
    return out.reshape(b, s, d)
```

```python
import functools
import math

import jax
import jax.numpy as jnp
from jax import lax
from jax.experimental import pallas as pl
from jax.experimental.pallas import tpu as pltpu

F32 = jnp.float32
BF16 = jnp.bfloat16
U32 = jnp.uint32
I32 = jnp.int32

DA_HEAD_DIM = 128
DA_HEADS = 4
REC_BLOCKS = 8
CONV_W = 4
RG_C = 8.0
CROSS_HEADS = 4
N_EXPERTS = 32
TOP_K = 4
SWIGLU_LIMIT = 7.0
SWIGLU_ALPHA = 1.702
EPS = 1e-6
DA_EPS = 1e-5
LAMBDA_INIT = 0.8 - 0.6 * math.exp(-0.3 * 0)
LOG2E = 1.4426950408889634

V7X_VMEM_BYTES = 64 * 1024 * 1024
VMEM_LIMIT = V7X_VMEM_BYTES - 8 * 1024 * 1024
EXPERTS_VMEM_LIMIT = V7X_VMEM_BYTES - 4 * 1024 * 1024

GROUP_ROWS = 1024
SUB_ROWS = 256
FF_TILE = 512


def _cparams(sem, vmem_limit=VMEM_LIMIT):
    return pltpu.CompilerParams(dimension_semantics=sem, vmem_limit_bytes=vmem_limit)


def _rms(x, g, eps):
    return x * lax.rsqrt(jnp.mean(x * x, axis=-1, keepdims=True) + eps) * g


def _norm_matmul_kernel(x_ref, g_ref, w_ref, cs_ref, o_ref, xn_ref):
    @pl.when(pl.program_id(1) == 0)
    def _():
        xn_ref[...] = _rms(x_ref[...], g_ref[...], EPS).astype(BF16)

    acc = jnp.dot(xn_ref[...], w_ref[...], preferred_element_type=F32)
    o_ref[...] = (acc * cs_ref[...]).astype(o_ref.dtype)


def _norm_matmul(x, g, w, col_scale, out_dtype, tm, tn):
    m, k = x.shape
    n = w.shape[1]
    tm, tn = min(tm, m), min(tn, n)
    return pl.pallas_call(
        _norm_matmul_kernel,
        grid=(m // tm, n // tn),
        in_specs=[pl.BlockSpec((tm, k), lambda i, j: (i, 0)),
                  pl.BlockSpec((1, k), lambda i, j: (0, 0)),
                  pl.BlockSpec((k, tn), lambda i, j: (0, j)),
                  pl.BlockSpec((1, tn), lambda i, j: (0, j))],
        out_specs=pl.BlockSpec((tm, tn), lambda i, j: (i, j)),
        out_shape=jax.ShapeDtypeStruct((m, n), out_dtype),
        scratch_shapes=[pltpu.VMEM((tm, k), BF16)],
        compiler_params=_cparams(("parallel", "arbitrary")),
        name="norm_matmul",
    )(x, g.reshape(1, k), w, col_scale)


def _in_proj_kernel(x_ref, g_ref, w_ref, cs_ref, qkv_ref, rg_ref, *, tn):
    xn = _rms(x_ref[...], g_ref[...], EPS).astype(BF16)
    n_qkv = qkv_ref.shape[1]
    for c0 in range(0, w_ref.shape[1], tn):
        acc = jnp.dot(xn, w_ref[:, c0:c0 + tn], preferred_element_type=F32)
        if c0 < n_qkv:
            qkv_ref[:, c0:c0 + tn] = (acc * cs_ref[:, c0:c0 + tn]).astype(qkv_ref.dtype)
        else:
            rg_ref[:, c0 - n_qkv:c0 - n_qkv + tn] = acc


def _in_proj(x, g, w, q_scale, n_qkv, tm, tn):
    m, k = x.shape
    n = w.shape[1]
    tm = min(tm, m)
    assert n_qkv % tn == 0 and (n - n_qkv) % tn == 0
    once = pl.Buffered(1)
    return pl.pallas_call(
        functools.partial(_in_proj_kernel, tn=tn),
        grid=(m // tm,),
        in_specs=[pl.BlockSpec((tm, k), lambda i: (i, 0)),
                  pl.BlockSpec((1, k), lambda i: (0, 0)),
                  pl.BlockSpec((k, n), lambda i: (0, 0), pipeline_mode=once),
                  pl.BlockSpec((1, n_qkv), lambda i: (0, 0))],
        out_specs=[pl.BlockSpec((tm, n_qkv), lambda i: (i, 0)),
                   pl.BlockSpec((tm, n - n_qkv), lambda i: (i, 0))],
        out_shape=[jax.ShapeDtypeStruct((m, n_qkv), BF16), jax.ShapeDtypeStruct((m, n - n_qkv), F32)],
        compiler_params=_cparams(("parallel",)),
        name="in_proj",
    )(x, g.reshape(1, k), w, q_scale)


def _matmul_res_kernel(a0_ref, a1_ref, w0_ref, w1_ref, r_ref, o_ref):
    hw = o_ref.shape[1] // 2
    for c in range(2):
        cols = slice(c * hw, (c + 1) * hw)
        acc = jnp.dot(a0_ref[...], w0_ref[:, cols], preferred_element_type=F32)
        acc += jnp.dot(a1_ref[...], w1_ref[:, cols], preferred_element_type=F32)
        o_ref[:, cols] = r_ref[:, cols] + acc


def _matmul_res(a0, c0, a1, c1, w, res, tm, tn):
    m, n = res.shape
    kh = w.shape[0] // 2
    tm, tn = min(tm, m), min(tn, n)
    w_mode = pl.Buffered(1) if tn == n else None
    return pl.pallas_call(
        _matmul_res_kernel,
        grid=(m // tm, n // tn),
        in_specs=[pl.BlockSpec((tm, kh), lambda i, j: (i, c0)),
                  pl.BlockSpec((tm, kh), lambda i, j: (i, c1)),
                  pl.BlockSpec((kh, tn), lambda i, j: (0, j), pipeline_mode=w_mode),
                  pl.BlockSpec((kh, tn), lambda i, j: (1, j), pipeline_mode=w_mode),
                  pl.BlockSpec((tm, tn), lambda i, j: (i, j))],
        out_specs=pl.BlockSpec((tm, tn), lambda i, j: (i, j)),
        out_shape=jax.ShapeDtypeStruct((m, n), F32),
        compiler_params=_cparams(("parallel", "parallel")),
        name="matmul_res",
    )(a0, a1, w, w, res)


def _diff_attn_kernel(qi_ref, ki_ref, q_ref, k_ref, v_ref, lam_ref, g_ref, o_ref,
                      m_ref, l_ref, acc_ref, *, t, n_split):
    p = pl.program_id(1)
    qi = qi_ref[p]
    ki = ki_ref[p]
    d = DA_HEAD_DIM
    rs = t // n_split

    @pl.when(ki == 0)
    def _():
        m_ref[...] = jnp.full(m_ref.shape, -jnp.inf, F32)
        l_ref[...] = jnp.zeros(l_ref.shape, F32)
        acc_ref[...] = jnp.zeros(acc_ref.shape, F32)

    def step(diag):
        for h in range(n_split):
            r0 = h * rs
            kw = r0 + rs if diag else t
            v = v_ref[0:kw, :]
            for i in range(2):
                q = q_ref[r0:r0 + rs, i * d:(i + 1) * d]
                k = k_ref[0:kw, i * d:(i + 1) * d]
                s = lax.dot_general(q, k, (((1,), (1,)), ((), ())), preferred_element_type=F32)
                if diag:
                    row = r0 + lax.broadcasted_iota(I32, (rs, kw), 0)
                    col = lax.broadcasted_iota(I32, (rs, kw), 1)
                    s = jnp.where(col <= row, s, -jnp.inf)
                m_old = m_ref[i, r0:r0 + rs, :1]
                m_new = jnp.maximum(m_old, jnp.max(s, axis=-1, keepdims=True))
                alpha = jnp.exp2(m_old - m_new)
                pr = jnp.exp2(s - m_new)
                l_ref[i, r0:r0 + rs, :] = jnp.broadcast_to(
                    alpha * l_ref[i, r0:r0 + rs, :1] + jnp.sum(pr, axis=-1, keepdims=True), (rs, 128))
                m_ref[i, r0:r0 + rs, :] = jnp.broadcast_to(m_new, (rs, 128))
                acc_ref[i, r0:r0 + rs, :] = alpha * acc_ref[i, r0:r0 + rs, :] + jnp.dot(
                    pr.astype(BF16), v, preferred_element_type=F32)

    @pl.when(ki == qi)
    def _():
        step(True)

    @pl.when(ki < qi)
    def _():
        step(False)

    @pl.when(ki == qi)
    def _():
        lp = lam_ref[...]
        lam = (jnp.exp(jnp.sum(lp[0:1] * lp[1:2], axis=-1, keepdims=True))
               - jnp.exp(jnp.sum(lp[2:3] * lp[3:4], axis=-1, keepdims=True)) + LAMBDA_INIT)
        o = acc_ref[0] / l_ref[0][:, :1] - lam * (acc_ref[1] / l_ref[1][:, :1])
        o = _rms(o, g_ref[...], DA_EPS) * (1.0 - LAMBDA_INIT)
        o_ref[...] = o.astype(o_ref.dtype)


def _diff_attn(qkv, lam_params, subln_g, t, n_split):
    s = qkv.shape[0]
    dv = 2 * DA_HEAD_DIM
    t = min(t, s)
    n_split = min(n_split, t // 128)
    assert s % t == 0 and t % n_split == 0
    nq = s // t
    pairs = [(a, b) for a in range(nq) for b in range(a + 1)]
    qi = jnp.asarray([a for a, _ in pairs], I32)
    ki = jnp.asarray([b for _, b in pairs], I32)
    grid_spec = pltpu.PrefetchScalarGridSpec(
        num_scalar_prefetch=2,
        grid=(DA_HEADS, len(pairs)),
        in_specs=[pl.BlockSpec((t, dv), lambda h, p, qi, ki: (qi[p], h)),
                  pl.BlockSpec((t, dv), lambda h, p, qi, ki: (ki[p], DA_HEADS + h)),
                  pl.BlockSpec((t, dv), lambda h, p, qi, ki: (ki[p], 2 * DA_HEADS + h)),
                  pl.BlockSpec((4, DA_HEAD_DIM), lambda h, p, qi, ki: (0, 0)),
                  pl.BlockSpec((1, dv), lambda h, p, qi, ki: (0, 0))],
        out_specs=pl.BlockSpec((t, dv), lambda h, p, qi, ki: (qi[p], h)),
        scratch_shapes=[pltpu.VMEM((2, t, 128), F32),
                        pltpu.VMEM((2, t, 128), F32),
                        pltpu.VMEM((2, t, dv), F32)],
    )
    return pl.pallas_call(
        functools.partial(_diff_attn_kernel, t=t, n_split=n_split),
        grid_spec=grid_spec,
        out_shape=jax.ShapeDtypeStruct((s, DA_HEADS * dv), BF16),
        compiler_params=_cparams(("parallel", "arbitrary")),
        name="diff_attn",
    )(qi, ki, qkv, qkv, qkv, lam_params, subln_g.reshape(1, dv))


def _rglru_kernel(xr_ref, xg_ref, cw_ref, cb_ref, wa_ref, ba_ref, wx_ref, bx_ref, lam_ref, g_ref,
                  o_ref, ext_ref, h_ref, *, t):
    @pl.when(pl.program_id(0) == 0)
    def _():
        ext_ref[0:8, :] = jnp.zeros((8, ext_ref.shape[1]), F32)
        h_ref[...] = jnp.zeros(h_ref.shape, F32)

    xr = xr_ref[...]
    c = xr.shape[1]
    bd = c // REC_BLOCKS
    ext_ref[8:8 + t, :] = xr
    cw = cw_ref[...]
    xc = xr * cw[CONV_W - 1:CONV_W] + cb_ref[...]
    for dlt in range(1, CONV_W):
        xc = xc + ext_ref[8 - dlt:8 - dlt + t, :] * cw[CONV_W - 1 - dlt:CONV_W - dlt]
    ext_ref[0:8, :] = xr[t - 8:t]

    xcb = xc.astype(BF16)
    ra, ri = [], []
    for n in range(REC_BLOCKS):
        xb = xcb[:, n * bd:(n + 1) * bd]
        ra.append(jnp.dot(xb, wa_ref[n], preferred_element_type=F32))
        ri.append(jnp.dot(xb, wx_ref[n], preferred_element_type=F32))
    r = jax.nn.sigmoid(jnp.concatenate(ra, axis=1) + ba_ref[...])
    ig = jax.nn.sigmoid(jnp.concatenate(ri, axis=1) + bx_ref[...])
    log_a = (-RG_C * jax.nn.softplus(-lam_ref[...])) * r
    a = jnp.exp(log_a)
    u = -jnp.tanh(log_a) * (a * a + 1.0)
    root = jnp.where(u > 0.0, u * lax.rsqrt(u), 0.0)
    b = root * (ig * xc)

    a = a.reshape(t // 8, 8, c)
    b = b.reshape(t // 8, 8, c)
    sub = lax.broadcasted_iota(I32, a.shape, 1)
    for dlt in (1, 2, 4):
        keep = sub < dlt
        a_sh = jnp.where(keep, 1.0, pltpu.roll(a, dlt, axis=1))
        b_sh = jnp.where(keep, 0.0, pltpu.roll(b, dlt, axis=1))
        b = a * b_sh + b
        a = a * a_sh
    h_in = h_ref[...]
    hs = []
    for j in range(t // 8):
        hj = a[j] * h_in + b[j]
        hs.append(hj)
        h_in = jnp.broadcast_to(hj[7:8], hj.shape)
    h = jnp.concatenate(hs, axis=0)
    h_ref[...] = h_in

    y = h * jax.nn.gelu(xg_ref[...])
    o_ref[...] = _rms(y, g_ref[...], EPS).astype(o_ref.dtype)


def _rglru(rg, conv_w, conv_b, w_a, b_a, w_x, b_x, rg_lambda, rec_norm_g, t):
    s = rg.shape[0]
    c = rg.shape[1] // 2
    t = min(t, s)
    bd = c // REC_BLOCKS
    vec = lambda: pl.BlockSpec((1, c), lambda i: (0, 0))
    return pl.pallas_call(
        functools.partial(_rglru_kernel, t=t),
        grid=(s // t,),
        in_specs=[pl.BlockSpec((t, c), lambda i: (i, 0)),
                  pl.BlockSpec((t, c), lambda i: (i, 1)),
                  pl.BlockSpec((CONV_W, c), lambda i: (0, 0)),
                  vec(),
                  pl.BlockSpec((REC_BLOCKS, bd, bd), lambda i: (0, 0, 0)),
                  vec(),
                  pl.BlockSpec((REC_BLOCKS, bd, bd), lambda i: (0, 0, 0)),
                  vec(), vec(), vec()],
        out_specs=pl.BlockSpec((t, c), lambda i: (i, 0)),
        out_shape=jax.ShapeDtypeStruct((s, c), BF16),
        scratch_shapes=[pltpu.VMEM((t + 8, c), F32), pltpu.VMEM((8, c), F32)],
        compiler_params=_cparams(("arbitrary",)),
        name="rglru",
    )(rg, rg, conv_w, conv_b.reshape(1, c), w_a.astype(BF16), b_a.reshape(1, c),
      w_x.astype(BF16), b_x.reshape(1, c), rg_lambda.reshape(1, c), rec_norm_g.reshape(1, c))


def _cross_attn_kernel(h_ref, g_ref, wq_ref, kv_ref, o_ref):
    d = h_ref.shape[1]
    hd = d // CROSS_HEADS
    c = (hd ** -0.5) * LOG2E
    n = _rms(h_ref[...], g_ref[...], EPS).astype(BF16)
    q = jnp.dot(n, wq_ref[...], preferred_element_type=F32).astype(BF16)
    for hh in range(CROSS_HEADS):
        qh = q[:, hh * hd:(hh + 1) * hd]
        kh = kv_ref[:, hh * hd:(hh + 1) * hd]
        vh = kv_ref[:, d + hh * hd:d + (hh + 1) * hd]
        s = lax.dot_general(qh, kh, (((1,), (1,)), ((), ())), preferred_element_type=F32) * c
        pr = jnp.exp2(s - jnp.max(s, axis=-1, keepdims=True))
        l = jnp.sum(pr, axis=-1, keepdims=True)
        o = jnp.dot(pr.astype(BF16), vh, preferred_element_type=F32) / l
        o_ref[:, hh * hd:(hh + 1) * hd] = o.astype(o_ref.dtype)


def _cross_attn(h, g, wq, kv, tm):
    s, d = h.shape
    nm = kv.shape[0]
    tm = min(tm, s)
    return pl.pallas_call(
        _cross_attn_kernel,
        grid=(s // tm,),
        in_specs=[pl.BlockSpec((tm, d), lambda i: (i, 0)),
                  pl.BlockSpec((1, d), lambda i: (0, 0)),
                  pl.BlockSpec((d, d), lambda i: (0, 0)),
                  pl.BlockSpec((nm, 2 * d), lambda i: (0, 0))],
        out_specs=pl.BlockSpec((tm, d), lambda i: (i, 0)),
        out_shape=jax.ShapeDtypeStruct((s, d), BF16),
        compiler_params=_cparams(("parallel",)),
        name="cross_attn",
    )(h, g.reshape(1, d), wq, kv)


def _router_kernel(h_ref, g_ref, wrt_ref, br_ref, tri_ref,
                   xp_ref, idx_ref, gate_ref, rank_ref, cnt_ref, carry_ref):
    @pl.when(pl.program_id(0) == 0)
    def _():
        carry_ref[...] = jnp.zeros(carry_ref.shape, F32)

    tm, d = h_ref.shape
    half = d // 2
    hn = _rms(h_ref[...], g_ref[...], EPS)
    hn_hi = hn.astype(BF16)
    hn_hi32 = hn_hi.astype(F32)
    bits = pltpu.bitcast(hn_hi32, U32)
    xp_ref[...] = (bits[:, :half] >> 16) | (bits[:, half:] & jnp.uint32(0xFFFF0000))

    hn_lo = (hn - hn_hi32).astype(BF16)

    def nt(a, b):
        return lax.dot_general(a, b, (((1,), (1,)), ((), ())), preferred_element_type=F32)

    logits = nt(wrt_ref[0], hn_hi) + nt(wrt_ref[0], hn_lo) + nt(wrt_ref[1], hn_hi) + br_ref[...]
    e_iota = lax.broadcasted_iota(I32, logits.shape, 0)
    vals, idxs, hits = [], [], []
    l = logits
    for _ in range(TOP_K):
        mx = jnp.max(l, axis=0, keepdims=True)
        ik = jnp.min(jnp.where(l == mx, e_iota, N_EXPERTS), axis=0, keepdims=True)
        hit = e_iota == ik
        vals.append(mx)
        idxs.append(ik)
        hits.append(hit)
        l = jnp.where(hit, -jnp.inf, l)
    ex = [jnp.exp(v - vals[0]) for v in vals]
    den = ex[0] + ex[1] + ex[2] + ex[3]
    gate_ref[...] = jnp.concatenate([e / den for e in ex], axis=0)
    idx_ref[...] = jnp.concatenate(idxs, axis=0)

    onehot = jnp.zeros(logits.shape, F32)
    for hit in hits:
        onehot = onehot + jnp.where(hit, 1.0, 0.0)
    before = jnp.dot(onehot.astype(BF16), tri_ref[...], preferred_element_type=F32)
    before = before + carry_ref[:, :1]
    ranks = [jnp.sum(jnp.where(hit, before, 0.0), axis=0, keepdims=True) for hit in hits]
    rank_ref[...] = jnp.concatenate(ranks, axis=0).astype(I32)
    carry = carry_ref[...] + jnp.sum(onehot, axis=1, keepdims=True)
    carry_ref[...] = carry
    cnt_ref[...] = carry


def _router(h, g, w_router, b_router, tm):
    s, d = h.shape
    tm = min(tm, s)
    tri = jnp.triu(jnp.ones((tm, tm), BF16), 1)
    w_t = w_router.T
    w_hi = w_t.astype(BF16)
    w_parts = jnp.stack([w_hi, (w_t - w_hi.astype(F32)).astype(BF16)])
    kt = lambda: pl.BlockSpec((TOP_K, tm), lambda i: (0, i))
    return pl.pallas_call(
        _router_kernel,
        grid=(s // tm,),
        in_specs=[pl.BlockSpec((tm, d), lambda i: (i, 0)),
                  pl.BlockSpec((1, d), lambda i: (0, 0)),
                  pl.BlockSpec((2, N_EXPERTS, d), lambda i: (0, 0, 0)),
                  pl.BlockSpec((N_EXPERTS, 1), lambda i: (0, 0)),
                  pl.BlockSpec((tm, tm), lambda i: (0, 0))],
        out_specs=[pl.BlockSpec((tm, d // 2), lambda i: (i, 0)), kt(), kt(), kt(),
                   pl.BlockSpec((N_EXPERTS, 128), lambda i: (0, 0))],
        out_shape=[jax.ShapeDtypeStruct((s, d // 2), U32),
                   jax.ShapeDtypeStruct((TOP_K, s), I32),
                   jax.ShapeDtypeStruct((TOP_K, s), F32),
                   jax.ShapeDtypeStruct((TOP_K, s), I32),
                   jax.ShapeDtypeStruct((N_EXPERTS, 128), F32)],
        scratch_shapes=[pltpu.VMEM((N_EXPERTS, 128), F32)],
        compiler_params=_cparams(("arbitrary",)),
        name="router",
    )(h, g.reshape(1, d), w_parts, b_router.reshape(N_EXPERTS, 1), tri)


def _experts_kernel(ge_ref, gx_ref, nsub_ref, nrow_ref,
                    x_ref, wg_ref, wu_ref, wd_ref, bg_ref, bu_ref, bd_ref, o_ref, xb_ref):
    g = pl.program_id(0)
    f = pl.program_id(1)
    ns = nsub_ref[g]
    half = x_ref.shape[1]

    def compute(rows, first):
        if first:
            w = x_ref[0:rows, :]
            valid = lax.broadcasted_iota(I32, w.shape, 0) < nrow_ref[g]
            lo = jnp.where(valid, pltpu.bitcast(w << 16, F32), 0.0).astype(BF16)
            hi = jnp.where(valid, pltpu.bitcast(w & jnp.uint32(0xFFFF0000), F32), 0.0).astype(BF16)
            xj = jnp.concatenate([lo, hi], axis=1)
            xb_ref[0:rows, :] = xj
        else:
            xj = xb_ref[0:rows, :]
        gate = jnp.dot(xj, wg_ref[0].astype(BF16), preferred_element_type=F32) + bg_ref[0]
        up = jnp.dot(xj, wu_ref[0].astype(BF16), preferred_element_type=F32) + bu_ref[0]
        gate = jnp.minimum(gate, SWIGLU_LIMIT)
        up = jnp.clip(up, -SWIGLU_LIMIT, SWIGLU_LIMIT)
        act = (up + 1.0) * gate * jax.nn.sigmoid(SWIGLU_ALPHA * gate)
        y = jnp.dot(act.astype(BF16), wd_ref[0].astype(BF16), preferred_element_type=F32)
        if first:
            o_ref[0:rows, :] = y + bd_ref[0]
            if rows < GROUP_ROWS:
                o_ref[rows:, :] = jnp.zeros((GROUP_ROWS - rows, o_ref.shape[1]), F32)
        else:
            o_ref[0:rows, :] += y

    for n in range(1, GROUP_ROWS // SUB_ROWS + 1):
        @pl.when(jnp.logical_and(ns == n, f == 0))
        def _(n=n):
            compute(n * SUB_ROWS, True)

        @pl.when(jnp.logical_and(ns == n, f > 0))
        def _(n=n):
            compute(n * SUB_ROWS, False)

    @pl.when(jnp.logical_and(ns == 0, f == 0))
    def _():
        o_ref[...] = jnp.zeros(o_ref.shape, F32)


def _experts(xs, w_gu, b_gu, w_down, b_down, g_e, g_x, g_nsub, g_nrow):
    m_pad, half = xs.shape
    d = 2 * half
    e, _, ff2 = w_gu.shape
    ff = ff2 // 2
    nf = ff // FF_TILE
    ng = m_pad // GROUP_ROWS

    def ft(g, f, ns):
        return jnp.where(ns[g] > 0, f, nf - 1)

    grid_spec = pltpu.PrefetchScalarGridSpec(
        num_scalar_prefetch=4,
        grid=(ng, nf),
        in_specs=[pl.BlockSpec((GROUP_ROWS, half), lambda g, f, ge, gx, ns, nr: (gx[g], 0)),
                  pl.BlockSpec((1, d, FF_TILE), lambda g, f, ge, gx, ns, nr: (ge[g], 0, ft(g, f, ns))),
                  pl.BlockSpec((1, d, FF_TILE), lambda g, f, ge, gx, ns, nr: (ge[g], 0, nf + ft(g, f, ns))),
                  pl.BlockSpec((1, FF_TILE, d), lambda g, f, ge, gx, ns, nr: (ge[g], ft(g, f, ns), 0)),
                  pl.BlockSpec((1, 1, FF_TILE), lambda g, f, ge, gx, ns, nr: (ge[g], 0, ft(g, f, ns))),
                  pl.BlockSpec((1, 1, FF_TILE), lambda g, f, ge, gx, ns, nr: (ge[g], 0, nf + ft(g, f, ns))),
                  pl.BlockSpec((1, 1, d), lambda g, f, ge, gx, ns, nr: (ge[g], 0, 0))],
        out_specs=pl.BlockSpec((GROUP_ROWS, d), lambda g, f, ge, gx, ns, nr: (g, 0)),
        scratch_shapes=[pltpu.VMEM((GROUP_ROWS, d), BF16)],
    )
    return pl.pallas_call(
        _experts_kernel,
        grid_spec=grid_spec,
        out_shape=jax.ShapeDtypeStruct((m_pad, d), F32),
        compiler_params=_cparams(("arbitrary", "arbitrary"), EXPERTS_VMEM_LIMIT),
        name="experts",
    )(g_e, g_x, g_nsub, g_nrow, xs, w_gu, w_gu, w_down,
      b_gu.reshape(e, 1, ff2), b_gu.reshape(e, 1, ff2), b_down.reshape(e, 1, d))


ROW_DMA_UNROLL = 8


def _for_rows(n_rows, fn):
    def body(r, carry):
        for k in range(TOP_K):
            fn(r, k)
        return carry
    lax.fori_loop(0, n_rows, body, 0, unroll=ROW_DMA_UNROLL)


ZERO_ROWS = 128


def _dispatch_kernel(dest_ref, nrow_ref, x_ref, xs_ref, zero_ref, sem, zsem):
    tm = x_ref.shape[0]

    @pl.when(pl.program_id(0) == 0)
    def _():
        zero_ref[...] = jnp.zeros(zero_ref.shape, U32)

        def chunk_copy(g, j):
            r0 = pl.multiple_of(g * GROUP_ROWS + j * ZERO_ROWS, ZERO_ROWS)
            return pltpu.make_async_copy(zero_ref, xs_ref.at[pl.ds(r0, ZERO_ROWS)], zsem)

        def for_chunks(fn):
            def body(g, carry):
                nr = nrow_ref[g]
                for j in range(GROUP_ROWS // ZERO_ROWS):
                    @pl.when(jnp.logical_and(nr > 0, (j + 1) * ZERO_ROWS > nr))
                    def _(j=j):
                        fn(g, j)
                return carry
            lax.fori_loop(0, nrow_ref.shape[0], body, 0)

        for_chunks(lambda g, j: chunk_copy(g, j).start())
        for_chunks(lambda g, j: chunk_copy(g, j).wait())

    def row_copy(r, k):
        return pltpu.make_async_copy(x_ref.at[pl.ds(r, 1)], xs_ref.at[pl.ds(dest_ref[k, r], 1)], sem)

    _for_rows(tm, lambda r, k: row_copy(r, k).start())
    _for_rows(tm, lambda r, k: row_copy(r, k).wait())


def _dispatch(xp, dest, g_nrow, m_pad, tm):
    s, w = xp.shape
    tm = min(tm, s)
    return pl.pallas_call(
        _dispatch_kernel,
        grid=(s // tm,),
        in_specs=[pl.BlockSpec((TOP_K, tm), lambda i: (0, i), memory_space=pltpu.SMEM),
                  pl.BlockSpec(memory_space=pltpu.SMEM),
                  pl.BlockSpec((tm, w), lambda i: (i, 0))],
        out_specs=pl.BlockSpec(memory_space=pl.ANY),
        out_shape=jax.ShapeDtypeStruct((m_pad, w), U32),
        scratch_shapes=[pltpu.VMEM((ZERO_ROWS, w), U32), pltpu.SemaphoreType.DMA, pltpu.SemaphoreType.DMA],
        compiler_params=_cparams(("arbitrary",)),
        name="dispatch",
    )(dest, g_nrow, xp)


def _combine_kernel(dest_ref, dnext_ref, gate_ref, h_ref, g_ref, y_ref, o_ref, buf_ref, sem):
    i = pl.program_id(0)
    tm = h_ref.shape[0]
    cur = lax.rem(i, 2)

    def row_copy(d_ref, b, r, k):
        return pltpu.make_async_copy(y_ref.at[pl.ds(d_ref[k, r], 1)], buf_ref.at[b, k, pl.ds(r, 1)], sem.at[b])

    @pl.when(i == 0)
    def _():
        _for_rows(tm, lambda r, k: row_copy(dest_ref, 0, r, k).start())

    @pl.when(i + 1 < pl.num_programs(0))
    def _():
        _for_rows(tm, lambda r, k: row_copy(dnext_ref, 1 - cur, r, k).start())

    _for_rows(tm, lambda r, k: row_copy(dest_ref, cur, r, k).wait())
    acc = h_ref[...]
    gates = gate_ref[...]
    for k in range(TOP_K):
        acc = acc + buf_ref[cur, k] * gates[:, k:k + 1]
    o_ref[...] = _rms(acc, g_ref[...], EPS)


def _combine(y, dest, gates_t, h, g, tm):
    s, d = h.shape
    tm = min(tm, s)
    nb = s // tm
    return pl.pallas_call(
        _combine_kernel,
        grid=(nb,),
        in_specs=[pl.BlockSpec((TOP_K, tm), lambda i: (0, i), memory_space=pltpu.SMEM),
                  pl.BlockSpec((TOP_K, tm), lambda i: (0, jnp.minimum(i + 1, nb - 1)), memory_space=pltpu.SMEM),
                  pl.BlockSpec((tm, TOP_K), lambda i: (i, 0)),
                  pl.BlockSpec((tm, d), lambda i: (i, 0)),
                  pl.BlockSpec((1, d), lambda i: (0, 0)),
                  pl.BlockSpec(memory_space=pl.ANY)],
        out_specs=pl.BlockSpec((tm, d), lambda i: (i, 0)),
        out_shape=jax.ShapeDtypeStruct((s, d), F32),
        scratch_shapes=[pltpu.VMEM((2, TOP_K, tm, d), F32), pltpu.SemaphoreType.DMA((2,))],
        compiler_params=_cparams(("arbitrary",)),
        name="combine",
    )(dest, dest, gates_t, h, g.reshape(1, d), y)


def _routing_tables(idx, rank, counts, n_groups):
    seg_groups = (counts + GROUP_ROWS - 1) // GROUP_ROWS
    grp_end = jnp.cumsum(seg_groups)
    grp_start = grp_end - seg_groups
    row_start = grp_start * GROUP_ROWS
    hit = idx[..., None] == jnp.arange(N_EXPERTS, dtype=I32)
    dest = jnp.sum(jnp.where(hit, row_start, 0), axis=-1) + rank
    total = grp_end[-1]
    gid = jnp.arange(n_groups, dtype=I32)
    used = gid < total
    gclamp = jnp.minimum(gid, total - 1)
    g_e = jnp.minimum(jnp.searchsorted(grp_end, gclamp, side="right"), N_EXPERTS - 1).astype(I32)
    rows = jnp.clip(counts[g_e] - (gclamp - grp_start[g_e]) * GROUP_ROWS, 0, GROUP_ROWS)
    g_nrow = jnp.where(used, rows, 0).astype(I32)
    g_nsub = (g_nrow + SUB_ROWS - 1) // SUB_ROWS
    return dest.astype(I32), g_e, gclamp.astype(I32), g_nsub.astype(I32), g_nrow


def kernel(x, mem, norm_mix_g, w_in, conv_w, conv_b, w_rg_a, b_rg_a, w_rg_x, b_rg_x, rg_lambda, rec_norm_g,
           lambda_q1, lambda_k1, lambda_q2, lambda_k2, subln_g, w_out, norm_cross_g, norm_mem_g, w_cq, w_ckv,
           w_co, norm_ffn_g, w_router, b_router, w_gate_up, b_gate_up, w_down, b_down, norm_final_g):
    b, s, d = x.shape
    assert b == 1 and w_in.shape[0] == 1
    aw = DA_HEADS * 2 * DA_HEAD_DIM
    h0 = x.reshape(s, d)

    w_in_b = w_in[0].astype(BF16)
    ones = lambda n: jnp.ones((1, n), F32)
    q_scale = jnp.concatenate([jnp.full((1, aw), (DA_HEAD_DIM ** -0.5) * LOG2E, F32), ones(2 * aw)], axis=1)
    qkv, rg = _in_proj(h0, norm_mix_g[0], w_in_b, q_scale, 3 * aw, 512, 1024)
    lam_params = jnp.stack([lambda_q1[0], lambda_k1[0], lambda_q2[0], lambda_k2[0]])
    attn = _diff_attn(qkv, lam_params, subln_g[0], 2048, 8)
    rec = _rglru(rg, conv_w[0], conv_b[0], w_rg_a[0], b_rg_a[0], w_rg_x[0], b_rg_x[0],
                 rg_lambda[0], rec_norm_g[0], 256)
    h1 = _matmul_res(attn, 0, rec, 0, w_out[0].astype(BF16), h0, 512, 2048)

    nm = mem.shape[1]
    kv = _norm_matmul(mem.reshape(nm, d), norm_mem_g[0], w_ckv[0].astype(BF16), ones(2 * d), BF16, 256, 1024)
    o = _cross_attn(h1, norm_cross_g[0], w_cq[0].astype(BF16), kv, 512)
    h2 = _matmul_res(o, 0, o, 1, w_co[0].astype(BF16), h1, 512, 2048)

    xp, idx, gates, rank, cnt = _router(h2, norm_ffn_g[0], w_router[0], b_router[0], 512)
    n_groups = (s * TOP_K) // GROUP_ROWS + N_EXPERTS
    dest, g_e, g_x, g_nsub, g_nrow = _routing_tables(idx, rank, cnt[:, 0].astype(I32), n_groups)
    xs = _dispatch(xp, dest, g_nrow, n_groups * GROUP_ROWS, 512)
    y = _experts(xs, w_gate_up[0], b_gate_up[0], w_down[0], b_down[0], g_e, g_x, g_nsub, g_nrow)
    out = _combine(y, dest, gates.T, h2, norm_final_g, 256)
    return out.reshape(b, s, d)
```
